```python
import math
import jax, jax.numpy as jnp
from jax import lax
import numpy as np

D_MODEL = 1024
BATCH = 8
SEQ = 8192
DEPTH = 4

D_MIX = D_MODEL
GM_HEADS = 4
GM_HEAD_DIM = D_MODEL // 16
GM_WIDTH = GM_HEADS * GM_HEAD_DIM
GM_CHUNK = 128
MLA_HEADS = 8
MLA_NOPE = 64
MLA_ROPE = 32
MLA_V = 64
MLA_WIDTH = MLA_HEADS * MLA_V
Q_LORA = 256
KV_LORA = 128
ROPE_BASE = 10000.0
Q_BLOCK = 128
SSM_GROUPS = 16
SSM_GROUP_CH = 16
SSM_WIDTH = SSM_GROUPS * SSM_GROUP_CH
SSM_STATE = 64
DT_MIN = 1e-3
DT_MAX = 1e-1
IN_COLS = 2 * GM_WIDTH + Q_LORA + KV_LORA + MLA_ROPE + SSM_WIDTH
D_FF = 2816
ALPHA = (2 * DEPTH) ** 0.25
BETA = (8 * DEPTH) ** -0.25
LN_EPS = 1e-5
RMS_EPS = 1e-6
NEG_BIG = -1e30

kernel_name = "hybrid_gmlp_mla_s5_macaron_deepnorm"


def layer_norm(x, g, b):
    xf = x.astype(jnp.float32)
    mu = jnp.mean(xf, axis=-1, keepdims=True)
    var = jnp.mean(jnp.square(xf - mu), axis=-1, keepdims=True)
    y = (xf - mu) * lax.rsqrt(var + LN_EPS) * g.astype(jnp.float32) + b.astype(jnp.float32)
    return y.astype(x.dtype)


def rms_norm(x, g):
    xf = x.astype(jnp.float32)
    y = xf * lax.rsqrt(jnp.mean(jnp.square(xf), axis=-1, keepdims=True) + RMS_EPS) * g.astype(jnp.float32)
    return y.astype(x.dtype)


def rms_only(x):
    xf = x.astype(jnp.float32)
    return (xf * lax.rsqrt(jnp.mean(jnp.square(xf), axis=-1, keepdims=True) + RMS_EPS)).astype(x.dtype)


def swiglu(x, w_gate, w_up, w_down):
    return (jax.nn.silu(x @ w_gate) * (x @ w_up)) @ w_down


def rope(x, cos, sin):
    half = x.shape[-1] // 2
    x1, x2 = x[..., :half], x[..., half:]
    cos = cos.astype(x.dtype)
    sin = sin.astype(x.dtype)
    return jnp.concatenate([x1 * cos - x2 * sin, x2 * cos + x1 * sin], axis=-1)


def gmlp_mixer(u, v, norm_g, ws, bs):
    b, s, _ = u.shape
    u = jax.nn.gelu(u)
    v = jax.nn.gelu(v).reshape(b, s, GM_HEADS, GM_HEAD_DIM)
    vf = v.astype(jnp.float32)
    mu = jnp.mean(vf, axis=-1, keepdims=True)
    var = jnp.mean(jnp.square(vf - mu), axis=-1, keepdims=True)
    v = ((vf - mu) * lax.rsqrt(var + LN_EPS) * norm_g.reshape(GM_HEADS, GM_HEAD_DIM).astype(jnp.float32)).astype(u.dtype)
    v = v.reshape(b, s // GM_CHUNK, GM_CHUNK, GM_HEADS, GM_HEAD_DIM)
    mask = jnp.tril(jnp.ones((GM_CHUNK, GM_CHUNK), dtype=ws.dtype))
    w_causal = ws * mask[None]
    z = jnp.einsum("bnchd,htc->bnthd", v, w_causal) + bs.T[:, :, None]
    return u * z.reshape(b, s, GM_WIDTH)


def mla_mixer(c_q, c_kv, k_rope_in, cos, sin, q_norm_g, w_uq, kv_norm_g, w_ukv):
    b, s, _ = c_q.shape
    q = (rms_norm(c_q, q_norm_g) @ w_uq).reshape(b, s, MLA_HEADS, MLA_NOPE + MLA_ROPE)
    q_nope = q[..., :MLA_NOPE]
    q_rope = rope(q[..., MLA_NOPE:], cos[:, :, None, :], sin[:, :, None, :])
    kv = (rms_norm(c_kv, kv_norm_g) @ w_ukv).reshape(b, s, MLA_HEADS, MLA_NOPE + MLA_V)
    k_nope = kv[..., :MLA_NOPE]
    v = kv[..., MLA_NOPE:]
    k_rope = rope(k_rope_in, cos, sin)
    nblk = s // Q_BLOCK
    qn_b = q_nope.reshape(b, nblk, Q_BLOCK, MLA_HEADS, MLA_NOPE).transpose(1, 0, 2, 3, 4)
    qr_b = q_rope.reshape(b, nblk, Q_BLOCK, MLA_HEADS, MLA_ROPE).transpose(1, 0, 2, 3, 4)
    scale = (MLA_NOPE + MLA_ROPE) ** -0.5
    kpos = jnp.arange(s)

    def block(args):
        qn, qr, i = args
        sc = jnp.einsum("bqhd,bkhd->bhqk", qn, k_nope) + jnp.einsum("bqhr,bkr->bhqk", qr, k_rope)
        sc = sc.astype(jnp.float32) * scale
        qpos = i * Q_BLOCK + jnp.arange(Q_BLOCK)
        causal = kpos[None, :] <= qpos[:, None]
        sc = jnp.where(causal[None, None], sc, NEG_BIG)
        p = jax.nn.softmax(sc, axis=-1).astype(v.dtype)
        return jnp.einsum("bhqk,bkhd->bqhd", p, v)

    o = lax.map(block, (qn_b, qr_b, jnp.arange(nblk)))
    return o.transpose(1, 0, 2, 3, 4).reshape(b, s, MLA_WIDTH)


def _complex_scan_combine(e1, e2):
    a1r, a1i, b1r, b1i = e1
    a2r, a2i, b2r, b2i = e2
    ar = a1r * a2r - a1i * a2i
    ai = a1r * a2i + a1i * a2r
    br = a2r * b1r - a2i * b1i + b2r
    bi = a2r * b1i + a2i * b1r + b2i
    return (ar, ai, br, bi)


def s5_mixer(u, a_re, a_im, b_re, b_im, c_re, c_im, d_skip, log_dt, glu_w, glu_b):
    bsz, s, _ = u.shape
    f32 = jnp.float32
    uf = u.astype(f32).reshape(bsz, s, SSM_GROUPS, SSM_GROUP_CH)
    ar, ai = a_re.astype(f32), a_im.astype(f32)
    dt = jnp.exp(log_dt.astype(f32))[:, None]
    mag = jnp.exp(ar * dt)
    abr = mag * jnp.cos(ai * dt)
    abi = mag * jnp.sin(ai * dt)
    den = ar * ar + ai * ai
    cr = ((abr - 1.0) * ar + abi * ai) / den
    ci = (abi * ar - (abr - 1.0) * ai) / den
    br, bi = b_re.astype(f32), b_im.astype(f32)
    bbr = cr[..., None] * br - ci[..., None] * bi
    bbi = cr[..., None] * bi + ci[..., None] * br
    bu_re = jnp.einsum("bsgc,gpc->bsgp", uf, bbr)
    bu_im = jnp.einsum("bsgc,gpc->bsgp", uf, bbi)
    a_seq_re = jnp.broadcast_to(abr[None, None], (1, s, SSM_GROUPS, SSM_STATE))
    a_seq_im = jnp.broadcast_to(abi[None, None], (1, s, SSM_GROUPS, SSM_STATE))
    _, _, h_re, h_im = lax.associative_scan(_complex_scan_combine, (a_seq_re, a_seq_im, bu_re, bu_im), axis=1)
    y = (jnp.einsum("bsgp,gcp->bsgc", h_re, c_re.astype(f32))
         - jnp.einsum("bsgp,gcp->bsgc", h_im, c_im.astype(f32))
         + d_skip.astype(f32) * uf)
    y = jax.nn.gelu(y).reshape(bsz, s, SSM_WIDTH)
    y = y * jax.nn.sigmoid(y @ glu_w.astype(f32) + glu_b.astype(f32))
    return y.astype(u.dtype)


def setup_inputs(seed: int = 0) -> dict:
    key = jax.random.key(seed)
    ks = jax.random.split(key, 32)
    f32 = jnp.float32
    L, D, F = DEPTH, D_MODEL, D_FF

    def nrm(k, shape, scale):
        return jax.random.normal(k, shape, f32) * scale

    x = jax.random.normal(ks[0], (BATCH, SEQ, D), f32)
    offset = jax.random.randint(ks[1], (BATCH, 1), 0, 1024, dtype=jnp.int32)
    positions = (offset + jnp.arange(SEQ, dtype=jnp.int32)[None, :]).astype(jnp.int32)
    ln_g = 1.0 + nrm(ks[2], (L, 3, D), 0.01)
    ln_b = nrm(ks[3], (L, 3, D), 0.01)
    ffn1_w_gate = nrm(ks[4], (L, D, F), D ** -0.5)
    ffn1_w_up = nrm(ks[5], (L, D, F), D ** -0.5)
    ffn1_w_down = nrm(ks[6], (L, F, D), BETA * F ** -0.5)
    w_in = nrm(ks[7], (L, D, IN_COLS), D ** -0.5)
    gmlp_norm_g = 1.0 + nrm(ks[8], (L, GM_WIDTH), 0.01)
    gmlp_ws = nrm(ks[9], (L, GM_HEADS, GM_CHUNK, GM_CHUNK), 0.5 * GM_CHUNK ** -0.5)
    gmlp_bs = 1.0 + nrm(ks[10], (L, GM_HEADS, GM_CHUNK), 0.01)
    mla_q_norm_g = 1.0 + nrm(ks[11], (L, Q_LORA), 0.01)
    mla_w_uq = nrm(ks[12], (L, Q_LORA, MLA_HEADS * (MLA_NOPE + MLA_ROPE)), Q_LORA ** -0.5)
    mla_kv_norm_g = 1.0 + nrm(ks[13], (L, KV_LORA), 0.01)
    mla_w_ukv = nrm(ks[14], (L, KV_LORA, MLA_HEADS * (MLA_NOPE + MLA_V)), KV_LORA ** -0.5)
    ssm_a_re = -0.5 + nrm(ks[15], (L, SSM_GROUPS, SSM_STATE), 0.01)
    ssm_a_im = (math.pi * jnp.arange(SSM_STATE, dtype=f32))[None, None, :] + nrm(ks[16], (L, SSM_GROUPS, SSM_STATE), 0.01)
    ssm_b_re = nrm(ks[17], (L, SSM_GROUPS, SSM_STATE, SSM_GROUP_CH), (2 * SSM_GROUP_CH) ** -0.5)
    ssm_b_im = nrm(ks[18], (L, SSM_GROUPS, SSM_STATE, SSM_GROUP_CH), (2 * SSM_GROUP_CH) ** -0.5)
    ssm_c_re = nrm(ks[19], (L, SSM_GROUPS, SSM_GROUP_CH, SSM_STATE), (2 * SSM_STATE) ** -0.5)
    ssm_c_im = nrm(ks[20], (L, SSM_GROUPS, SSM_GROUP_CH, SSM_STATE), (2 * SSM_STATE) ** -0.5)
    ssm_d = nrm(ks[21], (L, SSM_GROUPS, SSM_GROUP_CH), 1.0)
    ssm_log_dt = jax.random.uniform(ks[22], (L, SSM_GROUPS), f32, math.log(DT_MIN), math.log(DT_MAX))
    ssm_glu_w = nrm(ks[23], (L, SSM_WIDTH, SSM_WIDTH), SSM_WIDTH ** -0.5)
    ssm_glu_b = nrm(ks[24], (L, SSM_WIDTH), 0.01)
    mix_norm_g = 1.0 + nrm(ks[25], (L, D_MIX), 0.01)
    w_out = nrm(ks[26], (L, D_MIX, D), BETA * D_MIX ** -0.5)
    ffn2_w_gate = nrm(ks[27], (L, D, F), D ** -0.5)
    ffn2_w_up = nrm(ks[28], (L, D, F), D ** -0.5)
    ffn2_w_down = nrm(ks[29], (L, F, D), BETA * F ** -0.5)
    return {
        "x": x, "positions": positions, "ln_g": ln_g, "ln_b": ln_b,
        "ffn1_w_gate": ffn1_w_gate, "ffn1_w_up": ffn1_w_up, "ffn1_w_down": ffn1_w_down,
        "w_in": w_in,
        "gmlp_norm_g": gmlp_norm_g, "gmlp_ws": gmlp_ws, "gmlp_bs": gmlp_bs,
        "mla_q_norm_g": mla_q_norm_g, "mla_w_uq": mla_w_uq, "mla_kv_norm_g": mla_kv_norm_g, "mla_w_ukv": mla_w_ukv,
        "ssm_a_re": ssm_a_re, "ssm_a_im": ssm_a_im, "ssm_b_re": ssm_b_re, "ssm_b_im": ssm_b_im,
        "ssm_c_re": ssm_c_re, "ssm_c_im": ssm_c_im, "ssm_d": ssm_d, "ssm_log_dt": ssm_log_dt,
        "ssm_glu_w": ssm_glu_w, "ssm_glu_b": ssm_glu_b,
        "mix_norm_g": mix_norm_g, "w_out": w_out,
        "ffn2_w_gate": ffn2_w_gate, "ffn2_w_up": ffn2_w_up, "ffn2_w_down": ffn2_w_down,
    }


def reference(x, positions, ln_g, ln_b, ffn1_w_gate, ffn1_w_up, ffn1_w_down, w_in,
              gmlp_norm_g, gmlp_ws, gmlp_bs, mla_q_norm_g, mla_w_uq, mla_kv_norm_g, mla_w_ukv,
              ssm_a_re, ssm_a_im, ssm_b_re, ssm_b_im, ssm_c_re, ssm_c_im, ssm_d, ssm_log_dt,
              ssm_glu_w, ssm_glu_b, mix_norm_g, w_out, ffn2_w_gate, ffn2_w_up, ffn2_w_down):
    inv_freq = 1.0 / (ROPE_BASE ** (jnp.arange(0, MLA_ROPE, 2, dtype=jnp.float32) / MLA_ROPE))
    ang = positions.astype(jnp.float32)[..., None] * inv_freq
    cos, sin = jnp.cos(ang), jnp.sin(ang)
    o1 = 2 * GM_WIDTH
    o2 = o1 + Q_LORA
    o3 = o2 + KV_LORA
    o4 = o3 + MLA_ROPE
    for l in range(DEPTH):
        x = layer_norm(ALPHA * x + 0.5 * swiglu(x, ffn1_w_gate[l], ffn1_w_up[l], ffn1_w_down[l]), ln_g[l, 0], ln_b[l, 0])
        h = x @ w_in[l]
        y_a = gmlp_mixer(h[..., :GM_WIDTH], h[..., GM_WIDTH:o1], gmlp_norm_g[l], gmlp_ws[l], gmlp_bs[l])
        y_b = mla_mixer(h[..., o1:o2], h[..., o2:o3], h[..., o3:o4], cos, sin,
                        mla_q_norm_g[l], mla_w_uq[l], mla_kv_norm_g[l], mla_w_ukv[l])
        y_c = s5_mixer(h[..., o4:], ssm_a_re[l], ssm_a_im[l], ssm_b_re[l], ssm_b_im[l],
                       ssm_c_re[l], ssm_c_im[l], ssm_d[l], ssm_log_dt[l], ssm_glu_w[l], ssm_glu_b[l])
        y = jnp.concatenate([rms_only(y_a), rms_only(y_b), rms_only(y_c)], axis=-1) * mix_norm_g[l]
        x = layer_norm(ALPHA * x + y @ w_out[l], ln_g[l, 1], ln_b[l, 1])
        x = layer_norm(ALPHA * x + 0.5 * swiglu(x, ffn2_w_gate[l], ffn2_w_up[l], ffn2_w_down[l]), ln_g[l, 2], ln_b[l, 2])
    return x
```

```python
import functools
import math

import jax
import jax.numpy as jnp
from jax import lax
from jax.experimental import pallas as pl
from jax.experimental.pallas import tpu as pltpu

F32 = jnp.float32
BF16 = jnp.bfloat16

GM_HEADS = 4
GM_HEAD_DIM = 64
GM_WIDTH = GM_HEADS * GM_HEAD_DIM
GM_CHUNK = 128
MLA_HEADS = 8
MLA_NOPE = 64
MLA_ROPE = 32
MLA_V = 64
MLA_WIDTH = MLA_HEADS * MLA_V
Q_LORA = 256
KV_LORA = 128
ROPE_BASE = 10000.0
SSM_GROUPS = 16
SSM_GROUP_CH = 16
SSM_WIDTH = SSM_GROUPS * SSM_GROUP_CH
SSM_STATE = 64
SSM_NSTATE = SSM_GROUPS * SSM_STATE
LN_EPS = 1e-5
RMS_EPS = 1e-6
NEG_BIG = -1e30

LANES = 128
HEAD_PAD = LANES
VMEM_LIMIT = 56 * 1024 * 1024

ROW_TILE = 512
ATTN_TQ = 512
ATTN_TK = 512
SSM_TL = 64


def _gelu(x):
    c = math.sqrt(2.0 / math.pi)
    return 0.5 * x * (1.0 + jnp.tanh(c * (x + 0.044715 * (x * x * x))))


def _sigmoid(x):
    return 1.0 / (1.0 + jnp.exp(-x))


def _layer_norm(r, g, b):
    mu = jnp.mean(r, axis=-1, keepdims=True)
    d = r - mu
    var = jnp.mean(d * d, axis=-1, keepdims=True)
    return d * lax.rsqrt(var + LN_EPS) * g + b


def _rms_scale(y):
    return y * lax.rsqrt(jnp.mean(y * y, axis=-1, keepdims=True) + RMS_EPS)


def _dot(a, b):
    return jnp.dot(a, b, preferred_element_type=F32)


def _swiglu_ln(x, wg_ref, wu_ref, wd_ref, g, b, alpha, f_chunk):
    xb = x.astype(BF16)
    d_ff = wg_ref.shape[1]
    y = None
    for c0 in range(0, d_ff, f_chunk):
        hg = _dot(xb, wg_ref[:, c0:c0 + f_chunk])
        hu = _dot(xb, wu_ref[:, c0:c0 + f_chunk])
        act = (hg * _sigmoid(hg)) * hu
        part = _dot(act.astype(BF16), wd_ref[c0:c0 + f_chunk, :])
        y = part if y is None else y + part
    return _layer_norm(alpha * x + 0.5 * y, g, b)


def _ffn_ln_kernel(x_ref, wg_ref, wu_ref, wd_ref, g_ref, b_ref, o_ref, *, alpha, f_chunk):
    o_ref[...] = _swiglu_ln(x_ref[...], wg_ref, wu_ref, wd_ref, g_ref[...], b_ref[...], alpha, f_chunk)


def _resident(shape):
    nd = len(shape)
    return pl.BlockSpec(shape, lambda *_: (0,) * nd, pipeline_mode=pl.Buffered(1))


def _ffn_chunk(d_ff):
    return d_ff // 2 if (d_ff // 2) % LANES == 0 else d_ff


def _ffn_ln(x, wg, wu, wd, g, b, alpha):
    t, d = x.shape
    tm = ROW_TILE
    kern = functools.partial(_ffn_ln_kernel, alpha=alpha, f_chunk=_ffn_chunk(wg.shape[1]))
    return pl.pallas_call(
        kern,
        grid=(t // tm,),
        in_specs=[
            pl.BlockSpec((tm, d), lambda i: (i, 0)),
            _resident(wg.shape), _resident(wu.shape), _resident(wd.shape),
            _resident(g.shape), _resident(b.shape),
        ],
        out_specs=pl.BlockSpec((tm, d), lambda i: (i, 0)),
        out_shape=jax.ShapeDtypeStruct((t, d), F32),
        compiler_params=pltpu.CompilerParams(
            dimension_semantics=("arbitrary",), vmem_limit_bytes=VMEM_LIMIT),
        name="ffn_ln",
    )(x, wg, wu, wd, g, b)


def _rope_table_kernel(pos_ref, freq_ref, cos_ref, sin_ref):
    ang = pos_ref[...].astype(F32) * freq_ref[...]
    lane = lax.broadcasted_iota(jnp.int32, ang.shape, 1)
    half = MLA_ROPE // 2
    c = jnp.cos(ang)
    s = jnp.sin(ang)
    in_rope = (lane >= MLA_NOPE) & (lane < MLA_NOPE + MLA_ROPE)
    cos_ref[...] = jnp.where(lane < MLA_NOPE, 1.0, jnp.where(in_rope, c, 0.0))
    sin_ref[...] = jnp.where(in_rope, jnp.where(lane < MLA_NOPE + half, -s, s), 0.0)


def _rope_tables(positions):
    b, s = positions.shape
    t = b * s
    tm = ROW_TILE
    half = MLA_ROPE // 2
    inv_freq = 1.0 / (ROPE_BASE ** (jnp.arange(0, MLA_ROPE, 2, dtype=F32) / MLA_ROPE))
    freq_row = jnp.zeros((1, HEAD_PAD), F32)
    freq_row = freq_row.at[0, MLA_NOPE:MLA_NOPE + half].set(inv_freq)
    freq_row = freq_row.at[0, MLA_NOPE + half:MLA_NOPE + MLA_ROPE].set(inv_freq)
    pos = positions.reshape(t, 1)
    return pl.pallas_call(
        _rope_table_kernel,
        grid=(t // tm,),
        in_specs=[pl.BlockSpec((tm, 1), lambda i: (i, 0)),
                  pl.BlockSpec((1, HEAD_PAD), lambda i: (0, 0))],
        out_specs=[pl.BlockSpec((tm, HEAD_PAD), lambda i: (i, 0))] * 2,
        out_shape=[jax.ShapeDtypeStruct((t, HEAD_PAD), F32)] * 2,
        compiler_params=pltpu.CompilerParams(dimension_semantics=("arbitrary",)),
        name="rope_tables",
    )(pos, freq_row)


def _rope_tile(x, cos_t, sin_t, lane):
    half = MLA_ROPE // 2
    partner = jnp.where(lane < MLA_NOPE + half,
                        pltpu.roll(x, HEAD_PAD - half, 1),
                        pltpu.roll(x, half, 1))
    return x * cos_t + partner * sin_t


def _in_proj_kernel(x_ref, win_ref, gng_ref, gws_ref, gbias_ref, qg_ref, wuq_ref, kvg_ref,
                    wuk_ref, wuv_ref, cos_ref, sin_ref, gmix_a_ref,
                    ya_ref, q_ref, k_ref, v_ref, u_ref):
    tm = x_ref.shape[0]
    xb = x_ref[...].astype(BF16)
    h = _dot(xb, win_ref[...])
    o_v = GM_WIDTH
    o_q = 2 * GM_WIDTH
    o_kv = o_q + Q_LORA
    o_kr = o_kv + KV_LORA
    o_ssm = o_kr + HEAD_PAD

    u_ref[...] = h[:, o_ssm:o_ssm + SSM_WIDTH]

    ug = _gelu(h[:, 0:GM_WIDTH])
    vg = _gelu(h[:, o_v:o_v + GM_WIDTH])
    lane_g = lax.broadcasted_iota(jnp.int32, (1, GM_WIDTH), 1)
    head_masks = [(lane_g >= hd * GM_HEAD_DIM) & (lane_g < (hd + 1) * GM_HEAD_DIM)
                  for hd in range(GM_HEADS)]

    def seg_mean(a):
        out = jnp.zeros_like(a)
        for m in head_masks:
            s = jnp.sum(jnp.where(m, a, 0.0), axis=-1, keepdims=True) * (1.0 / GM_HEAD_DIM)
            out = jnp.where(m, s, out)
        return out

    dv = vg - seg_mean(vg)
    vn = dv * lax.rsqrt(seg_mean(dv * dv) + LN_EPS) * gng_ref[...]
    vnb = vn.astype(BF16)
    row = lax.broadcasted_iota(jnp.int32, (GM_CHUNK, GM_CHUNK), 0)
    col = lax.broadcasted_iota(jnp.int32, (GM_CHUNK, GM_CHUNK), 1)
    tril = col <= row
    wc = [jnp.where(tril, gws_ref[hd], 0.0).astype(BF16) for hd in range(GM_HEADS)]
    gbias = gbias_ref[...]
    gmix_a = gmix_a_ref[...]
    for c0 in range(0, tm, GM_CHUNK):
        vc = vnb[c0:c0 + GM_CHUNK, :]
        z = jnp.zeros((GM_CHUNK, GM_WIDTH), F32)
        for hd in range(GM_HEADS):
            z = jnp.where(head_masks[hd], _dot(wc[hd], vc), z)
        ya = ug[c0:c0 + GM_CHUNK, :] * (z + gbias)
        ya_ref[c0:c0 + GM_CHUNK, :] = (_rms_scale(ya) * gmix_a).astype(BF16)

    cos_t = cos_ref[...]
    sin_t = sin_ref[...]
    lane = lax.broadcasted_iota(jnp.int32, (tm, HEAD_PAD), 1)
    cq = h[:, o_q:o_q + Q_LORA]
    cqn = (_rms_scale(cq) * qg_ref[...]).astype(BF16)
    q = _dot(cqn, wuq_ref[...])
    for hd in range(MLA_HEADS):
        qh = q[:, hd * HEAD_PAD:(hd + 1) * HEAD_PAD]
        q_ref[:, hd * HEAD_PAD:(hd + 1) * HEAD_PAD] = _rope_tile(qh, cos_t, sin_t, lane).astype(BF16)
    ckv = h[:, o_kv:o_kv + KV_LORA]
    ckvn = (_rms_scale(ckv) * kvg_ref[...]).astype(BF16)
    kn = _dot(ckvn, wuk_ref[...])
    kr = _rope_tile(h[:, o_kr:o_kr + HEAD_PAD], cos_t, sin_t, lane)
    for hd in range(MLA_HEADS):
        k_ref[:, hd * HEAD_PAD:(hd + 1) * HEAD_PAD] = (kn[:, hd * HEAD_PAD:(hd + 1) * HEAD_PAD] + kr).astype(BF16)
    v_ref[...] = _dot(ckvn, wuv_ref[...]).astype(BF16)


def _in_proj(x, win, gng, gws, gbias, qg, wuq, kvg, wuk, wuv, cos_t, sin_t, gmix_a):
    t, d = x.shape
    tm = ROW_TILE
    row = lambda w: pl.BlockSpec((tm, w), lambda i: (i, 0))
    return pl.pallas_call(
        _in_proj_kernel,
        grid=(t // tm,),
        in_specs=[row(d), _resident(win.shape), _resident(gng.shape), _resident(gws.shape),
                  _resident(gbias.shape), _resident(qg.shape), _resident(wuq.shape),
                  _resident(kvg.shape), _resident(wuk.shape), _resident(wuv.shape),
                  row(HEAD_PAD), row(HEAD_PAD), _resident(gmix_a.shape)],
        out_specs=[row(GM_WIDTH), row(MLA_HEADS * HEAD_PAD), row(MLA_HEADS * HEAD_PAD),
                   row(MLA_WIDTH), row(SSM_WIDTH)],
        out_shape=[jax.ShapeDtypeStruct((t, GM_WIDTH), BF16),
                   jax.ShapeDtypeStruct((t, MLA_HEADS * HEAD_PAD), BF16),
                   jax.ShapeDtypeStruct((t, MLA_HEADS * HEAD_PAD), BF16),
                   jax.ShapeDtypeStruct((t, MLA_WIDTH), BF16),
                   jax.ShapeDtypeStruct((t, SSM_WIDTH), F32)],
        compiler_params=pltpu.CompilerParams(
            dimension_semantics=("arbitrary",), vmem_limit_bytes=VMEM_LIMIT),
        name="in_proj",
    )(x, win, gng, gws, gbias, qg, wuq, kvg, wuk, wuv, cos_t, sin_t, gmix_a)


def _attn_kernel(q_ref, k_ref, v_ref, o_ref, m_ref, l_ref, acc_ref, *, scale, tq, tk):
    qi = pl.program_id(2)
    n_full = qi * (tq // tk)
    row = lax.broadcasted_iota(jnp.int32, (tq, tk), 0)
    col = lax.broadcasted_iota(jnp.int32, (tq, tk), 1)
    m_ref[...] = jnp.full(m_ref.shape, NEG_BIG, F32)
    l_ref[...] = jnp.zeros(l_ref.shape, F32)
    acc_ref[...] = jnp.zeros(acc_ref.shape, F32)
    for hh in range(2):
        q = q_ref[0, :, hh * HEAD_PAD:(hh + 1) * HEAD_PAD]

        def step(j, diag_off, q=q, hh=hh):
            start = pl.multiple_of(j * tk, tk)
            kb = k_ref[0, pl.ds(start, tk), hh * HEAD_PAD:(hh + 1) * HEAD_PAD]
            s = lax.dot_general(q, kb, (((1,), (1,)), ((), ())), preferred_element_type=F32) * scale
            if diag_off is not None:
                s = jnp.where(col + diag_off <= row, s, NEG_BIG)
            m_old = m_ref[hh]
            m_new = jnp.maximum(m_old, jnp.max(s, axis=-1, keepdims=True))
            p = jnp.exp(s - m_new)
            alpha = jnp.exp(m_old - m_new)
            l_ref[hh] = alpha * l_ref[hh] + jnp.sum(p, axis=-1, keepdims=True)
            vb = v_ref[0, pl.ds(start, tk), :]
            acc_ref[hh] = alpha * acc_ref[hh] + _dot(p.astype(BF16), vb)
            m_ref[hh] = m_new

        def full_body(j, carry):
            step(j, None)
            return carry

        lax.fori_loop(0, n_full, full_body, 0)
        for dj in range(tq // tk):
            step(n_full + dj, dj * tk)
    lane = lax.broadcasted_iota(jnp.int32, (tq, 2 * MLA_V), 1)
    o_ref[0] = jnp.where(lane < MLA_V, acc_ref[0] / l_ref[0], acc_ref[1] / l_ref[1])


def _attention(q, k, v, batch, seq):
    tq, tk = ATTN_TQ, ATTN_TK
    scale = (MLA_NOPE + MLA_ROPE) ** -0.5
    q3 = q.reshape(batch, seq, MLA_HEADS * HEAD_PAD)
    k3 = k.reshape(batch, seq, MLA_HEADS * HEAD_PAD)
    v3 = v.reshape(batch, seq, MLA_WIDTH)
    kern = functools.partial(_attn_kernel, scale=scale, tq=tq, tk=tk)
    o = pl.pallas_call(
        kern,
        grid=(batch, MLA_HEADS // 2, seq // tq),
        in_specs=[pl.BlockSpec((1, tq, 2 * HEAD_PAD), lambda b, h, i: (b, i, h)),
                  pl.BlockSpec((1, seq, 2 * HEAD_PAD), lambda b, h, i: (b, 0, h)),
                  pl.BlockSpec((1, seq, 2 * MLA_V), lambda b, h, i: (b, 0, h))],
        out_specs=pl.BlockSpec((1, tq, 2 * MLA_V), lambda b, h, i: (b, i, h)),
        out_shape=jax.ShapeDtypeStruct((batch, seq, MLA_WIDTH), F32),
        scratch_shapes=[pltpu.VMEM((2, tq, 1), F32), pltpu.VMEM((2, tq, 1), F32),
                        pltpu.VMEM((2, tq, 2 * MLA_V), F32)],
        compiler_params=pltpu.CompilerParams(
            dimension_semantics=("arbitrary", "arbitrary", "arbitrary"),
            vmem_limit_bytes=VMEM_LIMIT),
        name="attention",
    )(q3, k3, v3)
    return o.reshape(batch * seq, MLA_WIDTH)


def _s5_prep_kernel(ar_ref, ai_ref, dt_ref, bre_ref, bim_ref, abr_ref, abi_ref, bb_ref):
    ar = ar_ref[...]
    ai = ai_ref[...]
    dt = jnp.exp(dt_ref[...])
    mag = jnp.exp(ar * dt)
    abr = mag * jnp.cos(ai * dt)
    abi = mag * jnp.sin(ai * dt)
    den = ar * ar + ai * ai
    cr = ((abr - 1.0) * ar + abi * ai) / den
    ci = (abi * ar - (abr - 1.0) * ai) / den
    abr_ref[...] = jnp.broadcast_to(abr, abr_ref.shape)
    abi_ref[...] = jnp.broadcast_to(abi, abi_ref.shape)
    bre = bre_ref[...]
    bim = bim_ref[...]
    r = lax.broadcasted_iota(jnp.int32, bre.shape, 0) // SSM_GROUP_CH
    c = lax.broadcasted_iota(jnp.int32, bre.shape, 1) // SSM_STATE
    same = r == c
    bb_ref[:, 0:SSM_NSTATE] = jnp.where(same, cr * bre - ci * bim, 0.0).astype(BF16)
    bb_ref[:, SSM_NSTATE:2 * SSM_NSTATE] = jnp.where(same, cr * bim + ci * bre, 0.0).astype(BF16)


def _s5_prep(a_re, a_im, log_dt, b_re, b_im, batch):
    n = SSM_NSTATE
    ar = a_re.reshape(1, n)
    ai = a_im.reshape(1, n)
    dt = jnp.repeat(log_dt, SSM_STATE).reshape(1, n)
    def expand(bm):
        cols = jnp.transpose(bm, (2, 0, 1)).reshape(SSM_GROUP_CH, n)
        return jnp.tile(cols, (SSM_GROUPS, 1))
    whole = lambda shp: pl.BlockSpec(shp, lambda: (0,) * len(shp))
    return pl.pallas_call(
        _s5_prep_kernel,
        in_specs=[whole((1, n)), whole((1, n)), whole((1, n)),
                  whole((SSM_WIDTH, n)), whole((SSM_WIDTH, n))],
        out_specs=[whole((batch, n)), whole((batch, n)), whole((SSM_WIDTH, 2 * n))],
        out_shape=[jax.ShapeDtypeStruct((batch, n), F32), jax.ShapeDtypeStruct((batch, n), F32),
                   jax.ShapeDtypeStruct((SSM_WIDTH, 2 * n), BF16)],
        name="s5_prep",
    )(ar, ai, dt, expand(b_re), expand(b_im))


def _s5_kernel(u_ref, abr_ref, abi_ref, bb_ref, cc_ref, d_ref, gw_ref, gb_ref, gmix_ref,
               y_ref, hre_ref, him_ref, cre_ref, cim_ref, *, batch, tl):
    n = SSM_NSTATE

    @pl.when(pl.program_id(0) == 0)
    def _():
        cre_ref[...] = jnp.zeros(cre_ref.shape, F32)
        cim_ref[...] = jnp.zeros(cim_ref.shape, F32)

    u = u_ref[...]
    bu = _dot(u.astype(BF16), bb_ref[...])
    hre_ref[...] = bu[:, 0:n]
    him_ref[...] = bu[:, n:2 * n]
    ar = abr_ref[...]
    ai = abi_ref[...]

    def step(t, carry):
        hr, hi = carry
        r0 = pl.multiple_of(t * batch, batch)
        nr = ar * hr - ai * hi + hre_ref[pl.ds(r0, batch), :]
        ni = ar * hi + ai * hr + him_ref[pl.ds(r0, batch), :]
        hre_ref[pl.ds(r0, batch), :] = nr
        him_ref[pl.ds(r0, batch), :] = ni
        return nr, ni

    hr, hi = lax.fori_loop(0, tl, step, (cre_ref[...], cim_ref[...]), unroll=4)
    cre_ref[...] = hr
    cim_ref[...] = hi

    y = (_dot(hre_ref[...].astype(BF16), cc_ref[0:n, :])
         + _dot(him_ref[...].astype(BF16), cc_ref[n:2 * n, :])
         + d_ref[...] * u)
    y = _gelu(y)
    y = y * _sigmoid(_dot(y.astype(BF16), gw_ref[...]) + gb_ref[...])
    y_ref[...] = (_rms_scale(y) * gmix_ref[...]).astype(BF16)


def _s5(u_tm, abr, abi, bb, cc, dskip, gw, gb, gmix_c, batch):
    rows, w = u_tm.shape
    tl = SSM_TL
    tr = tl * batch
    n = SSM_NSTATE
    kern = functools.partial(_s5_kernel, batch=batch, tl=tl)
    return pl.pallas_call(
        kern,
        grid=(rows // tr,),
        in_specs=[pl.BlockSpec((tr, w), lambda i: (i, 0)),
                  _resident(abr.shape), _resident(abi.shape), _resident(bb.shape),
                  _resident(cc.shape), _resident(dskip.shape), _resident(gw.shape),
                  _resident(gb.shape), _resident(gmix_c.shape)],
        out_specs=pl.BlockSpec((tr, w), lambda i: (i, 0)),
        out_shape=jax.ShapeDtypeStruct((rows, w), BF16),
        scratch_shapes=[pltpu.VMEM((tr, n), F32), pltpu.VMEM((tr, n), F32),
                        pltpu.VMEM((batch, n), F32), pltpu.VMEM((batch, n), F32)],
        compiler_params=pltpu.CompilerParams(
            dimension_semantics=("arbitrary",), vmem_limit_bytes=VMEM_LIMIT),
        name="s5",
    )(u_tm, abr, abi, bb, cc, dskip, gw, gb, gmix_c)


def _out_ffn_kernel(x_ref, ya_ref, o_ref, yc_ref, gmix_b_ref, wa_ref, wb_ref, wc_ref,
                    g1_ref, b1_ref, wg_ref, wu_ref, wd_ref, g2_ref, b2_ref, out_ref,
                    *, alpha, f_chunk):
    x = x_ref[...]
    yb = (_rms_scale(o_ref[...]) * gmix_b_ref[...]).astype(BF16)
    y = _dot(ya_ref[...], wa_ref[...]) + _dot(yb, wb_ref[...]) + _dot(yc_ref[...], wc_ref[...])
    x1 = _layer_norm(alpha * x + y, g1_ref[...], b1_ref[...])
    out_ref[...] = _swiglu_ln(x1, wg_ref, wu_ref, wd_ref, g2_ref[...], b2_ref[...], alpha, f_chunk)


def _out_ffn(x, ya, o, yc, gmix_b, wa, wb, wc, g1, b1, wg, wu, wd, g2, b2, alpha):
    t, d = x.shape
    tm = ROW_TILE
    row = lambda w: pl.BlockSpec((tm, w), lambda i: (i, 0))
    kern = functools.partial(_out_ffn_kernel, alpha=alpha, f_chunk=_ffn_chunk(wg.shape[1]))
    return pl.pallas_call(
        kern,
        grid=(t // tm,),
        in_specs=[row(d), row(GM_WIDTH), row(MLA_WIDTH), row(SSM_WIDTH),
                  _resident(gmix_b.shape), _resident(wa.shape), _resident(wb.shape),
                  _resident(wc.shape), _resident(g1.shape), _resident(b1.shape),
                  _resident(wg.shape), _resident(wu.shape), _resident(wd.shape),
                  _resident(g2.shape), _resident(b2.shape)],
        out_specs=row(d),
        out_shape=jax.ShapeDtypeStruct((t, d), F32),
        compiler_params=pltpu.CompilerParams(
            dimension_semantics=("arbitrary",), vmem_limit_bytes=VMEM_LIMIT),
        name="out_ffn",
    )(x, ya, o, yc, gmix_b, wa, wb, wc, g1, b1, wg, wu, wd, g2, b2)


def _pack_w_in(w_in):
    d = w_in.shape[0]
    o1 = 2 * GM_WIDTH
    o2 = o1 + Q_LORA
    o3 = o2 + KV_LORA
    o4 = o3 + MLA_ROPE
    kr_tile = jnp.zeros((d, HEAD_PAD), w_in.dtype).at[:, MLA_NOPE:MLA_NOPE + MLA_ROPE].set(w_in[:, o3:o4])
    return jnp.concatenate([w_in[:, :o3], kr_tile, w_in[:, o4:]], axis=1).astype(BF16)


def _pack_w_uq(w_uq):
    r = w_uq.shape[0]
    w = w_uq.reshape(r, MLA_HEADS, MLA_NOPE + MLA_ROPE)
    w = jnp.pad(w, ((0, 0), (0, 0), (0, HEAD_PAD - MLA_NOPE - MLA_ROPE)))
    return w.reshape(r, MLA_HEADS * HEAD_PAD).astype(BF16)


def _pack_w_ukv(w_ukv):
    r = w_ukv.shape[0]
    w = w_ukv.reshape(r, MLA_HEADS, MLA_NOPE + MLA_V)
    wk = jnp.pad(w[:, :, :MLA_NOPE], ((0, 0), (0, 0), (0, HEAD_PAD - MLA_NOPE)))
    wv = w[:, :, MLA_NOPE:]
    return (wk.reshape(r, MLA_HEADS * HEAD_PAD).astype(BF16),
            wv.reshape(r, MLA_WIDTH).astype(BF16))


def _pack_ssm_c(c_re, c_im):
    def blockdiag(cm):
        eye = jnp.eye(SSM_GROUPS, dtype=cm.dtype)
        m = jnp.transpose(cm, (0, 2, 1))[:, :, None, :] * eye[:, None, :, None]
        return m.reshape(SSM_NSTATE, SSM_WIDTH)
    return jnp.concatenate([blockdiag(c_re), -blockdiag(c_im)], axis=0).astype(BF16)


def kernel(x, positions, ln_g, ln_b, ffn1_w_gate, ffn1_w_up, ffn1_w_down, w_in, gmlp_norm_g, gmlp_ws, gmlp_bs, mla_q_norm_g, mla_w_uq, mla_kv_norm_g, mla_w_ukv, ssm_a_re, ssm_a_im, ssm_b_re, ssm_b_im, ssm_c_re, ssm_c_im, ssm_d, ssm_log_dt, ssm_glu_w, ssm_glu_b, mix_norm_g, w_out, ffn2_w_gate, ffn2_w_up, ffn2_w_down):
    batch, seq, d = x.shape
    depth = w_in.shape[0]
    t = batch * seq
    assert batch == 8, "the S5 kernel keeps the batch on the 8 sublanes of a vreg"
    assert seq % max(ROW_TILE, ATTN_TQ, SSM_TL) == 0 and ROW_TILE % GM_CHUNK == 0
    alpha = (2 * depth) ** 0.25

    cos_t, sin_t = _rope_tables(positions)
    xf = x.reshape(t, d)
    row = lambda a: a.reshape(1, -1)
    for l in range(depth):
        xf = _ffn_ln(xf, ffn1_w_gate[l].astype(BF16), ffn1_w_up[l].astype(BF16),
                     ffn1_w_down[l].astype(BF16), row(ln_g[l, 0]), row(ln_b[l, 0]), alpha)

        wk, wv = _pack_w_ukv(mla_w_ukv[l])
        gbias = jnp.repeat(gmlp_bs[l].T, GM_HEAD_DIM, axis=1)
        gmix = mix_norm_g[l]
        ya, q, k, v, u = _in_proj(
            xf, _pack_w_in(w_in[l]), row(gmlp_norm_g[l]), gmlp_ws[l], gbias,
            row(mla_q_norm_g[l]), _pack_w_uq(mla_w_uq[l]), row(mla_kv_norm_g[l]), wk, wv,
            cos_t, sin_t, row(gmix[:GM_WIDTH]))

        o = _attention(q, k, v, batch, seq)

        abr, abi, bb = _s5_prep(ssm_a_re[l], ssm_a_im[l], ssm_log_dt[l], ssm_b_re[l], ssm_b_im[l], batch)
        u_tm = jnp.transpose(u.reshape(batch, seq, SSM_WIDTH), (1, 0, 2)).reshape(t, SSM_WIDTH)
        yc_tm = _s5(u_tm, abr, abi, bb, _pack_ssm_c(ssm_c_re[l], ssm_c_im[l]), row(ssm_d[l]),
                    ssm_glu_w[l].astype(BF16), row(ssm_glu_b[l]),
                    row(gmix[GM_WIDTH + MLA_WIDTH:]), batch)
        yc = jnp.transpose(yc_tm.reshape(seq, batch, SSM_WIDTH), (1, 0, 2)).reshape(t, SSM_WIDTH)

        wo = w_out[l].astype(BF16)
        xf = _out_ffn(xf, ya, o, yc, row(gmix[GM_WIDTH:GM_WIDTH + MLA_WIDTH]),
                      wo[:GM_WIDTH], wo[GM_WIDTH:GM_WIDTH + MLA_WIDTH], wo[GM_WIDTH + MLA_WIDTH:],
                      row(ln_g[l, 1]), row(ln_b[l, 1]),
                      ffn2_w_gate[l].astype(BF16), ffn2_w_up[l].astype(BF16),
                      ffn2_w_down[l].astype(BF16), row(ln_g[l, 2]), row(ln_b[l, 2]), alpha)
    return xf.reshape(batch, seq, d)
```

```python
import functools
import math

import jax
import jax.numpy as jnp
from jax import lax
from jax.experimental import pallas as pl
from jax.experimental.pallas import tpu as pltpu

F32 = jnp.float32
BF16 = jnp.bfloat16

GM_HEADS = 4
GM_HEAD_DIM = 64
GM_WIDTH = GM_HEADS * GM_HEAD_DIM
GM_CHUNK = 128
MLA_HEADS = 8
MLA_NOPE = 64
MLA_ROPE = 32
MLA_V = 64
MLA_WIDTH = MLA_HEADS * MLA_V
Q_LORA = 256
KV_LORA = 128
ROPE_BASE = 10000.0
SSM_GROUPS = 16
SSM_GROUP_CH = 16
SSM_WIDTH = SSM_GROUPS * SSM_GROUP_CH
SSM_STATE = 64
SSM_NSTATE = SSM_GROUPS * SSM_STATE
LN_EPS = 1e-5
RMS_EPS = 1e-6
NEG_BIG = -1e30

LANES = 128
BF16_SUBLANES = 16
HEAD_PAD = LANES
MLA_VP = MLA_V + BF16_SUBLANES
SOFTMAX_C = (MLA_NOPE + MLA_ROPE) ** -0.5 * math.log2(math.e)
VMEM_LIMIT = 56 * 1024 * 1024

ROW_TILE = 512
ATTN_TQ = 512
ATTN_TK = 512
SSM_TL = 64


def _gelu(x):
    c = math.sqrt(2.0 / math.pi)
    return 0.5 * x * (1.0 + jnp.tanh(c * (x + 0.044715 * (x * x * x))))


def _sigmoid(x):
    return 1.0 / (1.0 + jnp.exp(-x))


def _layer_norm(r, g, b):
    mu = jnp.mean(r, axis=-1, keepdims=True)
    d = r - mu
    var = jnp.mean(d * d, axis=-1, keepdims=True)
    return d * lax.rsqrt(var + LN_EPS) * g + b


def _rms_scale(y):
    return y * lax.rsqrt(jnp.mean(y * y, axis=-1, keepdims=True) + RMS_EPS)


def _dot(a, b):
    return jnp.dot(a, b, preferred_element_type=F32)


def _swiglu_ln(x, wg_ref, wu_ref, wd_ref, g, b, alpha, f_chunk):
    xb = x.astype(BF16)
    d_ff = wg_ref.shape[1]
    y = None
    for c0 in range(0, d_ff, f_chunk):
        hg = _dot(xb, wg_ref[:, c0:c0 + f_chunk])
        hu = _dot(xb, wu_ref[:, c0:c0 + f_chunk])
        act = (hg * _sigmoid(hg)) * hu
        part = _dot(act.astype(BF16), wd_ref[c0:c0 + f_chunk, :])
        y = part if y is None else y + part
    return _layer_norm(alpha * x + 0.5 * y, g, b)


def _ffn_ln_kernel(x_ref, wg_ref, wu_ref, wd_ref, g_ref, b_ref, o_ref, *, alpha, f_chunk):
    o_ref[...] = _swiglu_ln(x_ref[...], wg_ref, wu_ref, wd_ref, g_ref[...], b_ref[...], alpha, f_chunk)


def _resident(shape):
    nd = len(shape)
    return pl.BlockSpec(shape, lambda *_: (0,) * nd, pipeline_mode=pl.Buffered(1))


def _ffn_chunk(d_ff):
    return d_ff // 2 if (d_ff // 2) % LANES == 0 else d_ff


def _ffn_ln(x, wg, wu, wd, g, b, alpha):
    t, d = x.shape
    tm = ROW_TILE
    kern = functools.partial(_ffn_ln_kernel, alpha=alpha, f_chunk=_ffn_chunk(wg.shape[1]))
    return pl.pallas_call(
        kern,
        grid=(t // tm,),
        in_specs=[
            pl.BlockSpec((tm, d), lambda i: (i, 0)),
            _resident(wg.shape), _resident(wu.shape), _resident(wd.shape),
            _resident(g.shape), _resident(b.shape),
        ],
        out_specs=pl.BlockSpec((tm, d), lambda i: (i, 0)),
        out_shape=jax.ShapeDtypeStruct((t, d), F32),
        compiler_params=pltpu.CompilerParams(
            dimension_semantics=("arbitrary",), vmem_limit_bytes=VMEM_LIMIT),
        name="ffn_ln",
    )(x, wg, wu, wd, g, b)


def _rope_table_kernel(pos_ref, freq_ref, cos_ref, sin_ref):
    ang = pos_ref[...].astype(F32) * freq_ref[...]
    lane = lax.broadcasted_iota(jnp.int32, ang.shape, 1)
    half = MLA_ROPE // 2
    c = jnp.cos(ang)
    s = jnp.sin(ang)
    in_rope = (lane >= MLA_NOPE) & (lane < MLA_NOPE + MLA_ROPE)
    cos_ref[...] = jnp.where(lane < MLA_NOPE, 1.0, jnp.where(in_rope, c, 0.0))
    sin_ref[...] = jnp.where(in_rope, jnp.where(lane < MLA_NOPE + half, -s, s), 0.0)


def _rope_tables(positions):
    b, s = positions.shape
    t = b * s
    tm = ROW_TILE
    half = MLA_ROPE // 2
    inv_freq = 1.0 / (ROPE_BASE ** (jnp.arange(0, MLA_ROPE, 2, dtype=F32) / MLA_ROPE))
    freq_row = jnp.zeros((1, HEAD_PAD), F32)
    freq_row = freq_row.at[0, MLA_NOPE:MLA_NOPE + half].set(inv_freq)
    freq_row = freq_row.at[0, MLA_NOPE + half:MLA_NOPE + MLA_ROPE].set(inv_freq)
    pos = positions.reshape(t, 1)
    return pl.pallas_call(
        _rope_table_kernel,
        grid=(t // tm,),
        in_specs=[pl.BlockSpec((tm, 1), lambda i: (i, 0)),
                  pl.BlockSpec((1, HEAD_PAD), lambda i: (0, 0))],
        out_specs=[pl.BlockSpec((tm, HEAD_PAD), lambda i: (i, 0))] * 2,
        out_shape=[jax.ShapeDtypeStruct((t, HEAD_PAD), F32)] * 2,
        compiler_params=pltpu.CompilerParams(dimension_semantics=("arbitrary",)),
        name="rope_tables",
    )(pos, freq_row)


def _rope_tile(x, cos_t, sin_t, lane):
    half = MLA_ROPE // 2
    partner = jnp.where(lane < MLA_NOPE + half,
                        pltpu.roll(x, HEAD_PAD - half, 1),
                        pltpu.roll(x, half, 1))
    return x * cos_t + partner * sin_t


def _in_proj_kernel(x_ref, win_ref, gng_ref, gws_ref, gbias_ref, qg_ref, wuq_ref, kvg_ref,
                    wuk_ref, wuvt_ref, vones_ref, cos_ref, sin_ref, gmix_a_ref,
                    ya_ref, q_ref, k_ref, vt_ref, u_ref):
    tm = x_ref.shape[0]
    xb = x_ref[...].astype(BF16)
    h = _dot(xb, win_ref[...])
    o_v = GM_WIDTH
    o_q = 2 * GM_WIDTH
    o_kv = o_q + Q_LORA
    o_kr = o_kv + KV_LORA
    o_ssm = o_kr + HEAD_PAD

    u_ref[...] = h[:, o_ssm:o_ssm + SSM_WIDTH]

    ug = _gelu(h[:, 0:GM_WIDTH])
    vg = _gelu(h[:, o_v:o_v + GM_WIDTH])
    lane_g = lax.broadcasted_iota(jnp.int32, (1, GM_WIDTH), 1)
    head_masks = [(lane_g >= hd * GM_HEAD_DIM) & (lane_g < (hd + 1) * GM_HEAD_DIM)
                  for hd in range(GM_HEADS)]

    def seg_mean(a):
        out = jnp.zeros_like(a)
        for m in head_masks:
            s = jnp.sum(jnp.where(m, a, 0.0), axis=-1, keepdims=True) * (1.0 / GM_HEAD_DIM)
            out = jnp.where(m, s, out)
        return out

    dv = vg - seg_mean(vg)
    vn = dv * lax.rsqrt(seg_mean(dv * dv) + LN_EPS) * gng_ref[...]
    vnb = vn.astype(BF16)
    row = lax.broadcasted_iota(jnp.int32, (GM_CHUNK, GM_CHUNK), 0)
    col = lax.broadcasted_iota(jnp.int32, (GM_CHUNK, GM_CHUNK), 1)
    tril = col <= row
    wc = [jnp.where(tril, gws_ref[hd], 0.0).astype(BF16) for hd in range(GM_HEADS)]
    gbias = gbias_ref[...]
    gmix_a = gmix_a_ref[...]
    for c0 in range(0, tm, GM_CHUNK):
        vc = vnb[c0:c0 + GM_CHUNK, :]
        z = jnp.zeros((GM_CHUNK, GM_WIDTH), F32)
        for hd in range(GM_HEADS):
            z = jnp.where(head_masks[hd], _dot(wc[hd], vc), z)
        ya = ug[c0:c0 + GM_CHUNK, :] * (z + gbias)
        ya_ref[c0:c0 + GM_CHUNK, :] = (_rms_scale(ya) * gmix_a).astype(BF16)

    cos_t = cos_ref[...]
    sin_t = sin_ref[...]
    lane = lax.broadcasted_iota(jnp.int32, (tm, HEAD_PAD), 1)
    cq = h[:, o_q:o_q + Q_LORA]
    cqn = (_rms_scale(cq) * qg_ref[...]).astype(BF16)
    q = _dot(cqn, wuq_ref[...])
    for hd in range(MLA_HEADS):
        qh = q[:, hd * HEAD_PAD:(hd + 1) * HEAD_PAD]
        q_ref[:, hd * HEAD_PAD:(hd + 1) * HEAD_PAD] = (
            _rope_tile(qh, cos_t, sin_t, lane) * SOFTMAX_C).astype(BF16)
    ckv = h[:, o_kv:o_kv + KV_LORA]
    ckvn = (_rms_scale(ckv) * kvg_ref[...]).astype(BF16)
    kn = _dot(ckvn, wuk_ref[...])
    kr = _rope_tile(h[:, o_kr:o_kr + HEAD_PAD], cos_t, sin_t, lane)
    for hd in range(MLA_HEADS):
        k_ref[:, hd * HEAD_PAD:(hd + 1) * HEAD_PAD] = (kn[:, hd * HEAD_PAD:(hd + 1) * HEAD_PAD] + kr).astype(BF16)
    vt = lax.dot_general(wuvt_ref[...], ckvn, (((1,), (1,)), ((), ())),
                         preferred_element_type=F32)
    vt_ref[0, 0] = (vt + vones_ref[...]).astype(BF16)


def _in_proj(x, win, gng, gws, gbias, qg, wuq, kvg, wuk, wuvt, cos_t, sin_t, gmix_a, batch, seq):
    t, d = x.shape
    tm = ROW_TILE
    spb = seq // tm
    row = lambda w: pl.BlockSpec((tm, w), lambda i: (i, 0))
    vrows = MLA_HEADS * MLA_VP
    is_one = (jnp.arange(vrows) % MLA_VP) == MLA_V
    vones = jnp.broadcast_to(is_one[:, None], (vrows, tm)).astype(F32)
    return pl.pallas_call(
        _in_proj_kernel,
        grid=(t // tm,),
        in_specs=[row(d), _resident(win.shape), _resident(gng.shape), _resident(gws.shape),
                  _resident(gbias.shape), _resident(qg.shape), _resident(wuq.shape),
                  _resident(kvg.shape), _resident(wuk.shape), _resident(wuvt.shape),
                  _resident(vones.shape), row(HEAD_PAD), row(HEAD_PAD), _resident(gmix_a.shape)],
        out_specs=[row(GM_WIDTH), row(MLA_HEADS * HEAD_PAD), row(MLA_HEADS * HEAD_PAD),
                   pl.BlockSpec((1, 1, vrows, tm), lambda i: (i // spb, i % spb, 0, 0)),
                   row(SSM_WIDTH)],
        out_shape=[jax.ShapeDtypeStruct((t, GM_WIDTH), BF16),
                   jax.ShapeDtypeStruct((t, MLA_HEADS * HEAD_PAD), BF16),
                   jax.ShapeDtypeStruct((t, MLA_HEADS * HEAD_PAD), BF16),
                   jax.ShapeDtypeStruct((batch, spb, vrows, tm), BF16),
                   jax.ShapeDtypeStruct((t, SSM_WIDTH), F32)],
        compiler_params=pltpu.CompilerParams(
            dimension_semantics=("arbitrary",), vmem_limit_bytes=VMEM_LIMIT),
        name="in_proj",
    )(x, win, gng, gws, gbias, qg, wuq, kvg, wuk, wuvt, vones, cos_t, sin_t, gmix_a)


def _attn_kernel(q_ref, k_ref, vt_ref, o_ref, s0_ref, s1_ref, bm0_ref, bm1_ref, m_ref, acc_ref, *, tq):
    tk = tq
    qi = pl.program_id(2)
    m_ref[...] = jnp.full(m_ref.shape, NEG_BIG, F32)
    acc_ref[...] = jnp.zeros(acc_ref.shape, F32)
    qs = [q_ref[0, :, hh * HEAD_PAD:(hh + 1) * HEAD_PAD] for hh in range(2)]

    def score(j, buf, masked):
        s_ref, bm_ref = buf
        start = pl.multiple_of(j * tk, tk)
        for hh in range(2):
            kb = k_ref[0, pl.ds(start, tk), hh * HEAD_PAD:(hh + 1) * HEAD_PAD]
            s = lax.dot_general(kb, qs[hh], (((1,), (1,)), ((), ())), preferred_element_type=F32)
            if masked:
                krow = lax.broadcasted_iota(jnp.int32, (tk, tq), 0)
                qcol = lax.broadcasted_iota(jnp.int32, (tk, tq), 1)
                s = jnp.where(krow <= qcol, s, NEG_BIG)
            s_ref[hh] = s
            bm_ref[hh] = jnp.max(s, axis=0, keepdims=True)

    def accumulate(j, buf):
        s_ref, bm_ref = buf
        for hh in range(2):
            m_old = m_ref[hh]
            m_new = jnp.maximum(m_old, bm_ref[hh])
            p = jnp.exp2(s_ref[hh] - m_new).astype(BF16)
            alpha = jnp.exp2(m_old - m_new)
            rows = slice(hh * MLA_VP, (hh + 1) * MLA_VP)
            vt = vt_ref[0, j, rows, :]
            acc_ref[rows, :] = alpha * acc_ref[rows, :] + _dot(vt, p)
            m_ref[hh] = m_new

    buf0 = (s0_ref, bm0_ref)
    buf1 = (s1_ref, bm1_ref)
    score(qi, buf0, True)

    def pair(t, carry):
        score(2 * t, buf1, False)
        accumulate(jnp.where(t == 0, qi, 2 * t - 1), buf0)
        score(2 * t + 1, buf0, False)
        accumulate(2 * t, buf1)
        return carry

    lax.fori_loop(0, qi // 2, pair, 0)
    odd = (qi & 1) == 1

    @pl.when(odd)
    def _():
        score(qi - 1, buf1, False)
        accumulate(jnp.where(qi == 1, qi, qi - 2), buf0)
        accumulate(qi - 1, buf1)

    @pl.when(jnp.logical_not(odd))
    def _():
        accumulate(jnp.where(qi == 0, qi, qi - 1), buf0)

    outs = []
    for hh in range(2):
        r0 = hh * MLA_VP
        inv_l = 1.0 / acc_ref[r0 + MLA_V:r0 + MLA_V + 1, :]
        outs.append(acc_ref[r0:r0 + MLA_V, :] * inv_l)
    o_ref[0] = jnp.concatenate(outs, axis=0).T


def _attention(q, k, vt, batch, seq):
    tq, tk = ATTN_TQ, ATTN_TK
    q3 = q.reshape(batch, seq, MLA_HEADS * HEAD_PAD)
    k3 = k.reshape(batch, seq, MLA_HEADS * HEAD_PAD)
    kern = functools.partial(_attn_kernel, tq=tq)
    o = pl.pallas_call(
        kern,
        grid=(batch, MLA_HEADS // 2, seq // tq),
        in_specs=[pl.BlockSpec((1, tq, 2 * HEAD_PAD), lambda b, h, i: (b, i, h)),
                  pl.BlockSpec((1, seq, 2 * HEAD_PAD), lambda b, h, i: (b, 0, h)),
                  pl.BlockSpec((1, seq // tk, 2 * MLA_VP, tk), lambda b, h, i: (b, 0, h, 0))],
        out_specs=pl.BlockSpec((1, tq, 2 * MLA_V), lambda b, h, i: (b, i, h)),
        out_shape=jax.ShapeDtypeStruct((batch, seq, MLA_WIDTH), F32),
        scratch_shapes=[pltpu.VMEM((2, tk, tq), F32), pltpu.VMEM((2, tk, tq), F32),
                        pltpu.VMEM((2, 1, tq), F32), pltpu.VMEM((2, 1, tq), F32),
                        pltpu.VMEM((2, 1, tq), F32), pltpu.VMEM((2 * MLA_VP, tq), F32)],
        compiler_params=pltpu.CompilerParams(
            dimension_semantics=("arbitrary", "arbitrary", "arbitrary"),
            vmem_limit_bytes=VMEM_LIMIT),
        name="attention",
    )(q3, k3, vt)
    return o.reshape(batch * seq, MLA_WIDTH)


def _s5_prep_kernel(ar_ref, ai_ref, dt_ref, bre_ref, bim_ref, abr_ref, abi_ref, bb_ref):
    ar = ar_ref[...]
    ai = ai_ref[...]
    dt = jnp.exp(dt_ref[...])
    mag = jnp.exp(ar * dt)
    abr = mag * jnp.cos(ai * dt)
    abi = mag * jnp.sin(ai * dt)
    den = ar * ar + ai * ai
    cr = ((abr - 1.0) * ar + abi * ai) / den
    ci = (abi * ar - (abr - 1.0) * ai) / den
    abr_ref[...] = jnp.broadcast_to(abr, abr_ref.shape)
    abi_ref[...] = jnp.broadcast_to(abi, abi_ref.shape)
    bre = bre_ref[...]
    bim = bim_ref[...]
    r = lax.broadcasted_iota(jnp.int32, bre.shape, 0) // SSM_GROUP_CH
    c = lax.broadcasted_iota(jnp.int32, bre.shape, 1) // SSM_STATE
    same = r == c
    bb_ref[:, 0:SSM_NSTATE] = jnp.where(same, cr * bre - ci * bim, 0.0).astype(BF16)
    bb_ref[:, SSM_NSTATE:2 * SSM_NSTATE] = jnp.where(same, cr * bim + ci * bre, 0.0).astype(BF16)


def _s5_prep(a_re, a_im, log_dt, b_re, b_im, batch):
    n = SSM_NSTATE
    ar = a_re.reshape(1, n)
    ai = a_im.reshape(1, n)
    dt = jnp.repeat(log_dt, SSM_STATE).reshape(1, n)
    def expand(bm):
        cols = jnp.transpose(bm, (2, 0, 1)).reshape(SSM_GROUP_CH, n)
        return jnp.tile(cols, (SSM_GROUPS, 1))
    whole = lambda shp: pl.BlockSpec(shp, lambda: (0,) * len(shp))
    return pl.pallas_call(
        _s5_prep_kernel,
        in_specs=[whole((1, n)), whole((1, n)), whole((1, n)),
                  whole((SSM_WIDTH, n)), whole((SSM_WIDTH, n))],
        out_specs=[whole((batch, n)), whole((batch, n)), whole((SSM_WIDTH, 2 * n))],
        out_shape=[jax.ShapeDtypeStruct((batch, n), F32), jax.ShapeDtypeStruct((batch, n), F32),
                   jax.ShapeDtypeStruct((SSM_WIDTH, 2 * n), BF16)],
        name="s5_prep",
    )(ar, ai, dt, expand(b_re), expand(b_im))


def _s5_kernel(u_ref, abr_ref, abi_ref, bb_ref, cc_ref, d_ref, gw_ref, gb_ref, gmix_ref,
               y_ref, hre_ref, him_ref, cre_ref, cim_ref, *, batch, tl):
    n = SSM_NSTATE

    @pl.when(pl.program_id(0) == 0)
    def _():
        cre_ref[...] = jnp.zeros(cre_ref.shape, F32)
        cim_ref[...] = jnp.zeros(cim_ref.shape, F32)

    u = u_ref[...]
    bu = _dot(u.astype(BF16), bb_ref[...])
    hre_ref[...] = bu[:, 0:n]
    him_ref[...] = bu[:, n:2 * n]
    ar = abr_ref[...]
    ai = abi_ref[...]

    def step(t, carry):
        hr, hi = carry
        r0 = pl.multiple_of(t * batch, batch)
        nr = ar * hr - ai * hi + hre_ref[pl.ds(r0, batch), :]
        ni = ar * hi + ai * hr + him_ref[pl.ds(r0, batch), :]
        hre_ref[pl.ds(r0, batch), :] = nr
        him_ref[pl.ds(r0, batch), :] = ni
        return nr, ni

    hr, hi = lax.fori_loop(0, tl, step, (cre_ref[...], cim_ref[...]), unroll=4)
    cre_ref[...] = hr
    cim_ref[...] = hi

    y = (_dot(hre_ref[...].astype(BF16), cc_ref[0:n, :])
         + _dot(him_ref[...].astype(BF16), cc_ref[n:2 * n, :])
         + d_ref[...] * u)
    y = _gelu(y)
    y = y * _sigmoid(_dot(y.astype(BF16), gw_ref[...]) + gb_ref[...])
    y_ref[...] = (_rms_scale(y) * gmix_ref[...]).astype(BF16)


def _s5(u_tm, abr, abi, bb, cc, dskip, gw, gb, gmix_c, batch):
    rows, w = u_tm.shape
    tl = SSM_TL
    tr = tl * batch
    n = SSM_NSTATE
    kern = functools.partial(_s5_kernel, batch=batch, tl=tl)
    return pl.pallas_call(
        kern,
        grid=(rows // tr,),
        in_specs=[pl.BlockSpec((tr, w), lambda i: (i, 0)),
                  _resident(abr.shape), _resident(abi.shape), _resident(bb.shape),
                  _resident(cc.shape), _resident(dskip.shape), _resident(gw.shape),
                  _resident(gb.shape), _resident(gmix_c.shape)],
        out_specs=pl.BlockSpec((tr, w), lambda i: (i, 0)),
        out_shape=jax.ShapeDtypeStruct((rows, w), BF16),
        scratch_shapes=[pltpu.VMEM((tr, n), F32), pltpu.VMEM((tr, n), F32),
                        pltpu.VMEM((batch, n), F32), pltpu.VMEM((batch, n), F32)],
        compiler_params=pltpu.CompilerParams(
            dimension_semantics=("arbitrary",), vmem_limit_bytes=VMEM_LIMIT),
        name="s5",
    )(u_tm, abr, abi, bb, cc, dskip, gw, gb, gmix_c)


def _out_ffn_kernel(x_ref, ya_ref, o_ref, yc_ref, gmix_b_ref, wa_ref, wb_ref, wc_ref,
                    g1_ref, b1_ref, wg_ref, wu_ref, wd_ref, g2_ref, b2_ref, out_ref,
                    *, alpha, f_chunk):
    x = x_ref[...]
    yb = (_rms_scale(o_ref[...]) * gmix_b_ref[...]).astype(BF16)
    y = _dot(ya_ref[...], wa_ref[...]) + _dot(yb, wb_ref[...]) + _dot(yc_ref[...], wc_ref[...])
    x1 = _layer_norm(alpha * x + y, g1_ref[...], b1_ref[...])
    out_ref[...] = _swiglu_ln(x1, wg_ref, wu_ref, wd_ref, g2_ref[...], b2_ref[...], alpha, f_chunk)


def _out_ffn(x, ya, o, yc, gmix_b, wa, wb, wc, g1, b1, wg, wu, wd, g2, b2, alpha):
    t, d = x.shape
    tm = ROW_TILE
    row = lambda w: pl.BlockSpec((tm, w), lambda i: (i, 0))
    kern = functools.partial(_out_ffn_kernel, alpha=alpha, f_chunk=_ffn_chunk(wg.shape[1]))
    return pl.pallas_call(
        kern,
        grid=(t // tm,),
        in_specs=[row(d), row(GM_WIDTH), row(MLA_WIDTH), row(SSM_WIDTH),
                  _resident(gmix_b.shape), _resident(wa.shape), _resident(wb.shape),
                  _resident(wc.shape), _resident(g1.shape), _resident(b1.shape),
                  _resident(wg.shape), _resident(wu.shape), _resident(wd.shape),
                  _resident(g2.shape), _resident(b2.shape)],
        out_specs=row(d),
        out_shape=jax.ShapeDtypeStruct((t, d), F32),
        compiler_params=pltpu.CompilerParams(
            dimension_semantics=("arbitrary",), vmem_limit_bytes=VMEM_LIMIT),
        name="out_ffn",
    )(x, ya, o, yc, gmix_b, wa, wb, wc, g1, b1, wg, wu, wd, g2, b2)


def _pack_w_in(w_in):
    d = w_in.shape[0]
    o1 = 2 * GM_WIDTH
    o2 = o1 + Q_LORA
    o3 = o2 + KV_LORA
    o4 = o3 + MLA_ROPE
    kr_tile = jnp.zeros((d, HEAD_PAD), w_in.dtype).at[:, MLA_NOPE:MLA_NOPE + MLA_ROPE].set(w_in[:, o3:o4])
    return jnp.concatenate([w_in[:, :o3], kr_tile, w_in[:, o4:]], axis=1).astype(BF16)


def _pack_w_uq(w_uq):
    r = w_uq.shape[0]
    w = w_uq.reshape(r, MLA_HEADS, MLA_NOPE + MLA_ROPE)
    w = jnp.pad(w, ((0, 0), (0, 0), (0, HEAD_PAD - MLA_NOPE - MLA_ROPE)))
    return w.reshape(r, MLA_HEADS * HEAD_PAD).astype(BF16)


def _pack_w_ukv(w_ukv):
    r = w_ukv.shape[0]
    w = w_ukv.reshape(r, MLA_HEADS, MLA_NOPE + MLA_V)
    wk = jnp.pad(w[:, :, :MLA_NOPE], ((0, 0), (0, 0), (0, HEAD_PAD - MLA_NOPE)))
    wvt = jnp.pad(jnp.transpose(w[:, :, MLA_NOPE:], (1, 2, 0)), ((0, 0), (0, MLA_VP - MLA_V), (0, 0)))
    return (wk.reshape(r, MLA_HEADS * HEAD_PAD).astype(BF16),
            wvt.reshape(MLA_HEADS * MLA_VP, r).astype(BF16))


def _pack_ssm_c(c_re, c_im):
    def blockdiag(cm):
        eye = jnp.eye(SSM_GROUPS, dtype=cm.dtype)
        m = jnp.transpose(cm, (0, 2, 1))[:, :, None, :] * eye[:, None, :, None]
        return m.reshape(SSM_NSTATE, SSM_WIDTH)
    return jnp.concatenate([blockdiag(c_re), -blockdiag(c_im)], axis=0).astype(BF16)


def kernel(x, positions, ln_g, ln_b, ffn1_w_gate, ffn1_w_up, ffn1_w_down, w_in, gmlp_norm_g, gmlp_ws, gmlp_bs, mla_q_norm_g, mla_w_uq, mla_kv_norm_g, mla_w_ukv, ssm_a_re, ssm_a_im, ssm_b_re, ssm_b_im, ssm_c_re, ssm_c_im, ssm_d, ssm_log_dt, ssm_glu_w, ssm_glu_b, mix_norm_g, w_out, ffn2_w_gate, ffn2_w_up, ffn2_w_down):
    batch, seq, d = x.shape
    depth = w_in.shape[0]
    t = batch * seq
    assert batch == 8, "the S5 kernel keeps the batch on the 8 sublanes of a vreg"
    assert seq % max(ROW_TILE, ATTN_TQ, SSM_TL) == 0 and ROW_TILE % GM_CHUNK == 0
    assert ATTN_TQ == ATTN_TK == ROW_TILE
    alpha = (2 * depth) ** 0.25

    cos_t, sin_t = _rope_tables(positions)
    xf = x.reshape(t, d)
    row = lambda a: a.reshape(1, -1)
    for l in range(depth):
        xf = _ffn_ln(xf, ffn1_w_gate[l].astype(BF16), ffn1_w_up[l].astype(BF16),
                     ffn1_w_down[l].astype(BF16), row(ln_g[l, 0]), row(ln_b[l, 0]), alpha)

        wk, wvt = _pack_w_ukv(mla_w_ukv[l])
        gbias = jnp.repeat(gmlp_bs[l].T, GM_HEAD_DIM, axis=1)
        gmix = mix_norm_g[l]
        ya, q, k, vt, u = _in_proj(
            xf, _pack_w_in(w_in[l]), row(gmlp_norm_g[l]), gmlp_ws[l], gbias,
            row(mla_q_norm_g[l]), _pack_w_uq(mla_w_uq[l]), row(mla_kv_norm_g[l]), wk, wvt,
            cos_t, sin_t, row(gmix[:GM_WIDTH]), batch, seq)

        o = _attention(q, k, vt, batch, seq)

        abr, abi, bb = _s5_prep(ssm_a_re[l], ssm_a_im[l], ssm_log_dt[l], ssm_b_re[l], ssm_b_im[l], batch)
        u_tm = jnp.transpose(u.reshape(batch, seq, SSM_WIDTH), (1, 0, 2)).reshape(t, SSM_WIDTH)
        yc_tm = _s5(u_tm, abr, abi, bb, _pack_ssm_c(ssm_c_re[l], ssm_c_im[l]), row(ssm_d[l]),
                    ssm_glu_w[l].astype(BF16), row(ssm_glu_b[l]),
                    row(gmix[GM_WIDTH + MLA_WIDTH:]), batch)
        yc = jnp.transpose(yc_tm.reshape(seq, batch, SSM_WIDTH), (1, 0, 2)).reshape(t, SSM_WIDTH)

        wo = w_out[l].astype(BF16)
        xf = _out_ffn(xf, ya, o, yc, row(gmix[GM_WIDTH:GM_WIDTH + MLA_WIDTH]),
                      wo[:GM_WIDTH], wo[GM_WIDTH:GM_WIDTH + MLA_WIDTH], wo[GM_WIDTH + MLA_WIDTH:],
                      row(ln_g[l, 1]), row(ln_b[l, 1]),
                      ffn2_w_gate[l].astype(BF16), ffn2_w_up[l].astype(BF16),
                      ffn2_w_down[l].astype(BF16), row(ln_g[l, 2]), row(ln_b[l, 2]), alpha)
    return xf.reshape(batch, seq, d)
```

```python
import functools
import math

import jax
import jax.numpy as jnp
from jax import lax
from jax.experimental import pallas as pl
from jax.experimental.pallas import tpu as pltpu

F32 = jnp.float32
BF16 = jnp.bfloat16

GM_HEADS = 4
GM_HEAD_DIM = 64
GM_WIDTH = GM_HEADS * GM_HEAD_DIM
GM_CHUNK = 128
MLA_HEADS = 8
MLA_NOPE = 64
MLA_ROPE = 32
MLA_V = 64
MLA_WIDTH = MLA_HEADS * MLA_V
Q_LORA = 256
KV_LORA = 128
ROPE_BASE = 10000.0
SSM_GROUPS = 16
SSM_GROUP_CH = 16
SSM_WIDTH = SSM_GROUPS * SSM_GROUP_CH
SSM_STATE = 64
SSM_NSTATE = SSM_GROUPS * SSM_STATE
LN_EPS = 1e-5
RMS_EPS = 1e-6
NEG_BIG = -1e30

LANES = 128
BF16_SUBLANES = 16
HEAD_PAD = LANES
MLA_VP = MLA_V + BF16_SUBLANES
SOFTMAX_C = (MLA_NOPE + MLA_ROPE) ** -0.5 * math.log2(math.e)
VMEM_LIMIT = 56 * 1024 * 1024

ROW_TILE = 512
ATTN_TQ = 512
ATTN_TK = 512
ATTN_CHUNK = 256
ATTN_UNROLL = 2
SSM_TL = 64


def _gelu(x):
    c = math.sqrt(2.0 / math.pi)
    return 0.5 * x * (1.0 + jnp.tanh(c * (x + 0.044715 * (x * x * x))))


def _sigmoid(x):
    return 1.0 / (1.0 + jnp.exp(-x))


def _layer_norm(r, g, b):
    mu = jnp.mean(r, axis=-1, keepdims=True)
    d = r - mu
    var = jnp.mean(d * d, axis=-1, keepdims=True)
    return d * lax.rsqrt(var + LN_EPS) * g + b


def _rms_scale(y):
    return y * lax.rsqrt(jnp.mean(y * y, axis=-1, keepdims=True) + RMS_EPS)


def _dot(a, b):
    return jnp.dot(a, b, preferred_element_type=F32)


def _swiglu_ln(x, wg_ref, wu_ref, wd_ref, g, b, alpha, f_chunk):
    xb = x.astype(BF16)
    d_ff = wg_ref.shape[1]
    y = None
    for c0 in range(0, d_ff, f_chunk):
        hg = _dot(xb, wg_ref[:, c0:c0 + f_chunk])
        hu = _dot(xb, wu_ref[:, c0:c0 + f_chunk])
        act = (hg * _sigmoid(hg)) * hu
        part = _dot(act.astype(BF16), wd_ref[c0:c0 + f_chunk, :])
        y = part if y is None else y + part
    return _layer_norm(alpha * x + 0.5 * y, g, b)


def _ffn_ln_kernel(x_ref, wg_ref, wu_ref, wd_ref, g_ref, b_ref, o_ref, *, alpha, f_chunk):
    o_ref[...] = _swiglu_ln(x_ref[...], wg_ref, wu_ref, wd_ref, g_ref[...], b_ref[...], alpha, f_chunk)


def _resident(shape):
    nd = len(shape)
    return pl.BlockSpec(shape, lambda *_: (0,) * nd, pipeline_mode=pl.Buffered(1))


def _ffn_chunk(d_ff):
    return d_ff // 2 if (d_ff // 2) % LANES == 0 else d_ff


def _ffn_ln(x, wg, wu, wd, g, b, alpha):
    t, d = x.shape
    tm = ROW_TILE
    kern = functools.partial(_ffn_ln_kernel, alpha=alpha, f_chunk=_ffn_chunk(wg.shape[1]))
    return pl.pallas_call(
        kern,
        grid=(t // tm,),
        in_specs=[
            pl.BlockSpec((tm, d), lambda i: (i, 0)),
            _resident(wg.shape), _resident(wu.shape), _resident(wd.shape),
            _resident(g.shape), _resident(b.shape),
        ],
        out_specs=pl.BlockSpec((tm, d), lambda i: (i, 0)),
        out_shape=jax.ShapeDtypeStruct((t, d), F32),
        compiler_params=pltpu.CompilerParams(
            dimension_semantics=("arbitrary",), vmem_limit_bytes=VMEM_LIMIT),
        name="ffn_ln",
    )(x, wg, wu, wd, g, b)


def _rope_table_kernel(pos_ref, freq_ref, cos_ref, sin_ref):
    ang = pos_ref[...].astype(F32) * freq_ref[...]
    lane = lax.broadcasted_iota(jnp.int32, ang.shape, 1)
    half = MLA_ROPE // 2
    c = jnp.cos(ang)
    s = jnp.sin(ang)
    in_rope = (lane >= MLA_NOPE) & (lane < MLA_NOPE + MLA_ROPE)
    cos_ref[...] = jnp.where(lane < MLA_NOPE, 1.0, jnp.where(in_rope, c, 0.0))
    sin_ref[...] = jnp.where(in_rope, jnp.where(lane < MLA_NOPE + half, -s, s), 0.0)


def _rope_tables(positions):
    b, s = positions.shape
    t = b * s
    tm = ROW_TILE
    half = MLA_ROPE // 2
    inv_freq = 1.0 / (ROPE_BASE ** (jnp.arange(0, MLA_ROPE, 2, dtype=F32) / MLA_ROPE))
    freq_row = jnp.zeros((1, HEAD_PAD), F32)
    freq_row = freq_row.at[0, MLA_NOPE:MLA_NOPE + half].set(inv_freq)
    freq_row = freq_row.at[0, MLA_NOPE + half:MLA_NOPE + MLA_ROPE].set(inv_freq)
    pos = positions.reshape(t, 1)
    return pl.pallas_call(
        _rope_table_kernel,
        grid=(t // tm,),
        in_specs=[pl.BlockSpec((tm, 1), lambda i: (i, 0)),
                  pl.BlockSpec((1, HEAD_PAD), lambda i: (0, 0))],
        out_specs=[pl.BlockSpec((tm, HEAD_PAD), lambda i: (i, 0))] * 2,
        out_shape=[jax.ShapeDtypeStruct((t, HEAD_PAD), F32)] * 2,
        compiler_params=pltpu.CompilerParams(dimension_semantics=("arbitrary",)),
        name="rope_tables",
    )(pos, freq_row)


def _rope_tile(x, cos_t, sin_t, lane):
    half = MLA_ROPE // 2
    partner = jnp.where(lane < MLA_NOPE + half,
                        pltpu.roll(x, HEAD_PAD - half, 1),
                        pltpu.roll(x, half, 1))
    return x * cos_t + partner * sin_t


def _in_proj_kernel(x_ref, win_ref, gng_ref, gws_ref, gbias_ref, qg_ref, wuq_ref, kvg_ref,
                    wuk_ref, wuvt_ref, vones_ref, cos_ref, sin_ref, gmix_a_ref,
                    ya_ref, q_ref, k_ref, vt_ref, u_ref):
    tm = x_ref.shape[0]
    xb = x_ref[...].astype(BF16)
    h = _dot(xb, win_ref[...])
    o_v = GM_WIDTH
    o_q = 2 * GM_WIDTH
    o_kv = o_q + Q_LORA
    o_kr = o_kv + KV_LORA
    o_ssm = o_kr + HEAD_PAD

    u_ref[...] = h[:, o_ssm:o_ssm + SSM_WIDTH]

    ug = _gelu(h[:, 0:GM_WIDTH])
    vg = _gelu(h[:, o_v:o_v + GM_WIDTH])
    lane_g = lax.broadcasted_iota(jnp.int32, (1, GM_WIDTH), 1)
    head_masks = [(lane_g >= hd * GM_HEAD_DIM) & (lane_g < (hd + 1) * GM_HEAD_DIM)
                  for hd in range(GM_HEADS)]

    def seg_mean(a):
        out = jnp.zeros_like(a)
        for m in head_masks:
            s = jnp.sum(jnp.where(m, a, 0.0), axis=-1, keepdims=True) * (1.0 / GM_HEAD_DIM)
            out = jnp.where(m, s, out)
        return out

    dv = vg - seg_mean(vg)
    vn = dv * lax.rsqrt(seg_mean(dv * dv) + LN_EPS) * gng_ref[...]
    vnb = vn.astype(BF16)
    row = lax.broadcasted_iota(jnp.int32, (GM_CHUNK, GM_CHUNK), 0)
    col = lax.broadcasted_iota(jnp.int32, (GM_CHUNK, GM_CHUNK), 1)
    tril = col <= row
    wc = [jnp.where(tril, gws_ref[hd], 0.0).astype(BF16) for hd in range(GM_HEADS)]
    gbias = gbias_ref[...]
    gmix_a = gmix_a_ref[...]
    for c0 in range(0, tm, GM_CHUNK):
        vc = vnb[c0:c0 + GM_CHUNK, :]
        z = jnp.zeros((GM_CHUNK, GM_WIDTH), F32)
        for hd in range(GM_HEADS):
            z = jnp.where(head_masks[hd], _dot(wc[hd], vc), z)
        ya = ug[c0:c0 + GM_CHUNK, :] * (z + gbias)
        ya_ref[c0:c0 + GM_CHUNK, :] = (_rms_scale(ya) * gmix_a).astype(BF16)

    cos_t = cos_ref[...]
    sin_t = sin_ref[...]
    lane = lax.broadcasted_iota(jnp.int32, (tm, HEAD_PAD), 1)
    cq = h[:, o_q:o_q + Q_LORA]
    cqn = (_rms_scale(cq) * qg_ref[...]).astype(BF16)
    q = _dot(cqn, wuq_ref[...])
    for hd in range(MLA_HEADS):
        qh = q[:, hd * HEAD_PAD:(hd + 1) * HEAD_PAD]
        q_ref[:, hd * HEAD_PAD:(hd + 1) * HEAD_PAD] = (
            _rope_tile(qh, cos_t, sin_t, lane) * SOFTMAX_C).astype(BF16)
    ckv = h[:, o_kv:o_kv + KV_LORA]
    ckvn = (_rms_scale(ckv) * kvg_ref[...]).astype(BF16)
    kn = _dot(ckvn, wuk_ref[...])
    kr = _rope_tile(h[:, o_kr:o_kr + HEAD_PAD], cos_t, sin_t, lane)
    for hd in range(MLA_HEADS):
        k_ref[:, hd * HEAD_PAD:(hd + 1) * HEAD_PAD] = (kn[:, hd * HEAD_PAD:(hd + 1) * HEAD_PAD] + kr).astype(BF16)
    vt = lax.dot_general(wuvt_ref[...], ckvn, (((1,), (1,)), ((), ())),
                         preferred_element_type=F32)
    vt_ref[0, 0] = (vt + vones_ref[...]).astype(BF16)


def _in_proj(x, win, gng, gws, gbias, qg, wuq, kvg, wuk, wuvt, cos_t, sin_t, gmix_a, batch, seq):
    t, d = x.shape
    tm = ROW_TILE
    spb = seq // tm
    row = lambda w: pl.BlockSpec((tm, w), lambda i: (i, 0))
    vrows = MLA_HEADS * MLA_VP
    is_one = (jnp.arange(vrows) % MLA_VP) == MLA_V
    vones = jnp.broadcast_to(is_one[:, None], (vrows, tm)).astype(F32)
    return pl.pallas_call(
        _in_proj_kernel,
        grid=(t // tm,),
        in_specs=[row(d), _resident(win.shape), _resident(gng.shape), _resident(gws.shape),
                  _resident(gbias.shape), _resident(qg.shape), _resident(wuq.shape),
                  _resident(kvg.shape), _resident(wuk.shape), _resident(wuvt.shape),
                  _resident(vones.shape), row(HEAD_PAD), row(HEAD_PAD), _resident(gmix_a.shape)],
        out_specs=[row(GM_WIDTH), row(MLA_HEADS * HEAD_PAD), row(MLA_HEADS * HEAD_PAD),
                   pl.BlockSpec((1, 1, vrows, tm), lambda i: (i // spb, i % spb, 0, 0)),
                   pl.BlockSpec((tm, SSM_WIDTH), lambda i: (i % spb, i // spb))],
        out_shape=[jax.ShapeDtypeStruct((t, GM_WIDTH), BF16),
                   jax.ShapeDtypeStruct((t, MLA_HEADS * HEAD_PAD), BF16),
                   jax.ShapeDtypeStruct((t, MLA_HEADS * HEAD_PAD), BF16),
                   jax.ShapeDtypeStruct((batch, spb, vrows, tm), BF16),
                   jax.ShapeDtypeStruct((seq, batch * SSM_WIDTH), F32)],
        compiler_params=pltpu.CompilerParams(
            dimension_semantics=("arbitrary",), vmem_limit_bytes=VMEM_LIMIT),
        name="in_proj",
    )(x, win, gng, gws, gbias, qg, wuq, kvg, wuk, wuvt, vones, cos_t, sin_t, gmix_a)


def _attn_kernel(q_ref, k_ref, vt_ref, o_ref, s0_ref, s1_ref, bm0_ref, bm1_ref, m_ref, acc_ref, *, tq):
    tk = tq
    qi = pl.program_id(2)
    m_ref[...] = jnp.full(m_ref.shape, NEG_BIG, F32)
    acc_ref[...] = jnp.zeros(acc_ref.shape, F32)
    qs = [q_ref[0, :, hh * HEAD_PAD:(hh + 1) * HEAD_PAD] for hh in range(2)]

    def stage(score_to, acc_from, masked=False):
        if acc_from is not None:
            ja, (sa_ref, bma_ref) = acc_from
            m_old = [m_ref[hh] for hh in range(2)]
            m_new = [jnp.maximum(m_old[hh], bma_ref[hh]) for hh in range(2)]
        if score_to is not None:
            js, (ss_ref, bms_ref) = score_to
            start = pl.multiple_of(js * tk, tk)
            bmax = [None, None]
        order = [(kind, c0) for c0 in range(0, tk, ATTN_CHUNK) for kind in ("s", "a")]
        for kind, c0 in order:
            for hh in range(2):
                rows = slice(hh * MLA_VP, (hh + 1) * MLA_VP)
                if kind == "s" and score_to is not None:
                    kb = k_ref[0, pl.ds(start + c0, ATTN_CHUNK), hh * HEAD_PAD:(hh + 1) * HEAD_PAD]
                    s = lax.dot_general(kb, qs[hh], (((1,), (1,)), ((), ())),
                                        preferred_element_type=F32)
                    if masked:
                        krow = lax.broadcasted_iota(jnp.int32, s.shape, 0) + c0
                        qcol = lax.broadcasted_iota(jnp.int32, s.shape, 1)
                        s = jnp.where(krow <= qcol, s, NEG_BIG)
                    ss_ref[hh, c0:c0 + ATTN_CHUNK, :] = s
                    cmax = jnp.max(s, axis=0, keepdims=True)
                    bmax[hh] = cmax if bmax[hh] is None else jnp.maximum(bmax[hh], cmax)
                if kind == "a" and acc_from is not None:
                    p = jnp.exp2(sa_ref[hh, c0:c0 + ATTN_CHUNK, :] - m_new[hh]).astype(BF16)
                    d = _dot(vt_ref[0, ja, rows, c0:c0 + ATTN_CHUNK], p)
                    if c0 == 0:
                        acc_ref[rows, :] = jnp.exp2(m_old[hh] - m_new[hh]) * acc_ref[rows, :] + d
                    else:
                        acc_ref[rows, :] += d
        for hh in range(2):
            if acc_from is not None:
                m_ref[hh] = m_new[hh]
            if score_to is not None:
                bms_ref[hh] = bmax[hh]

    bufs = ((s0_ref, bm0_ref), (s1_ref, bm1_ref))
    unroll = ATTN_UNROLL
    stage((qi, bufs[0]), None, masked=True)

    def pipelined(j0, r):
        prev = jnp.where(j0 == 0, qi, j0 - 1) if r == 0 else j0 + r - 1
        stage((j0 + r, bufs[(r + 1) % 2]), (prev, bufs[r % 2]))

    def group(t, carry):
        for r in range(unroll):
            pipelined(t * unroll, r)
        return carry

    n_groups = qi // unroll
    lax.fori_loop(0, n_groups, group, 0)
    rem = qi - n_groups * unroll
    for r in range(unroll - 1):
        pl.when(rem > r)(functools.partial(pipelined, n_groups * unroll, r))
    last = jnp.where(qi == 0, qi, qi - 1)
    for parity in range(2):
        pl.when((rem & 1) == parity)(functools.partial(stage, None, (last, bufs[parity])))

    outs = []
    for hh in range(2):
        r0 = hh * MLA_VP
        inv_l = 1.0 / acc_ref[r0 + MLA_V:r0 + MLA_V + 1, :]
        outs.append(acc_ref[r0:r0 + MLA_V, :] * inv_l)
    o_ref[0] = jnp.concatenate(outs, axis=0).T


def _attention(q, k, vt, batch, seq):
    tq, tk = ATTN_TQ, ATTN_TK
    q3 = q.reshape(batch, seq, MLA_HEADS * HEAD_PAD)
    k3 = k.reshape(batch, seq, MLA_HEADS * HEAD_PAD)
    kern = functools.partial(_attn_kernel, tq=tq)
    o = pl.pallas_call(
        kern,
        grid=(batch, MLA_HEADS // 2, seq // tq),
        in_specs=[pl.BlockSpec((1, tq, 2 * HEAD_PAD), lambda b, h, i: (b, i, h)),
                  pl.BlockSpec((1, seq, 2 * HEAD_PAD), lambda b, h, i: (b, 0, h)),
                  pl.BlockSpec((1, seq // tk, 2 * MLA_VP, tk), lambda b, h, i: (b, 0, h, 0))],
        out_specs=pl.BlockSpec((1, tq, 2 * MLA_V), lambda b, h, i: (b, i, h)),
        out_shape=jax.ShapeDtypeStruct((batch, seq, MLA_WIDTH), F32),
        scratch_shapes=[pltpu.VMEM((2, tk, tq), F32), pltpu.VMEM((2, tk, tq), F32),
                        pltpu.VMEM((2, 1, tq), F32), pltpu.VMEM((2, 1, tq), F32),
                        pltpu.VMEM((2, 1, tq), F32), pltpu.VMEM((2 * MLA_VP, tq), F32)],
        compiler_params=pltpu.CompilerParams(
            dimension_semantics=("arbitrary", "arbitrary", "arbitrary"),
            vmem_limit_bytes=VMEM_LIMIT),
        name="attention",
    )(q3, k3, vt)
    return o.reshape(batch * seq, MLA_WIDTH)


def _s5_prep_kernel(ar_ref, ai_ref, dt_ref, bre_ref, bim_ref, abr_ref, abi_ref, bb_ref):
    ar = ar_ref[...]
    ai = ai_ref[...]
    dt = jnp.exp(dt_ref[...])
    mag = jnp.exp(ar * dt)
    abr = mag * jnp.cos(ai * dt)
    abi = mag * jnp.sin(ai * dt)
    den = ar * ar + ai * ai
    cr = ((abr - 1.0) * ar + abi * ai) / den
    ci = (abi * ar - (abr - 1.0) * ai) / den
    abr_ref[...] = jnp.broadcast_to(abr, abr_ref.shape)
    abi_ref[...] = jnp.broadcast_to(abi, abi_ref.shape)
    bre = bre_ref[...]
    bim = bim_ref[...]
    r = lax.broadcasted_iota(jnp.int32, bre.shape, 0) // SSM_GROUP_CH
    c = lax.broadcasted_iota(jnp.int32, bre.shape, 1) // SSM_STATE
    same = r == c
    bb_ref[:, 0:SSM_NSTATE] = jnp.where(same, cr * bre - ci * bim, 0.0).astype(BF16)
    bb_ref[:, SSM_NSTATE:2 * SSM_NSTATE] = jnp.where(same, cr * bim + ci * bre, 0.0).astype(BF16)


def _s5_prep(a_re, a_im, log_dt, b_re, b_im, batch):
    n = SSM_NSTATE
    ar = a_re.reshape(1, n)
    ai = a_im.reshape(1, n)
    dt = jnp.repeat(log_dt, SSM_STATE).reshape(1, n)
    def expand(bm):
        cols = jnp.transpose(bm, (2, 0, 1)).reshape(SSM_GROUP_CH, n)
        return jnp.tile(cols, (SSM_GROUPS, 1))
    whole = lambda shp: pl.BlockSpec(shp, lambda: (0,) * len(shp))
    return pl.pallas_call(
        _s5_prep_kernel,
        in_specs=[whole((1, n)), whole((1, n)), whole((1, n)),
                  whole((SSM_WIDTH, n)), whole((SSM_WIDTH, n))],
        out_specs=[whole((batch, n)), whole((batch, n)), whole((SSM_WIDTH, 2 * n))],
        out_shape=[jax.ShapeDtypeStruct((batch, n), F32), jax.ShapeDtypeStruct((batch, n), F32),
                   jax.ShapeDtypeStruct((SSM_WIDTH, 2 * n), BF16)],
        name="s5_prep",
    )(ar, ai, dt, expand(b_re), expand(b_im))


def _s5_kernel(u_ref, abr_ref, abi_ref, bb_ref, cc_ref, d_ref, gw_ref, gb_ref, gmix_ref,
               y_ref, hre_ref, him_ref, cre_ref, cim_ref, *, batch, tl):
    n = SSM_NSTATE

    @pl.when(pl.program_id(0) == 0)
    def _():
        cre_ref[...] = jnp.zeros(cre_ref.shape, F32)
        cim_ref[...] = jnp.zeros(cim_ref.shape, F32)

    u = u_ref[...]
    bu = _dot(u.astype(BF16), bb_ref[...])
    hre_ref[...] = bu[:, 0:n]
    him_ref[...] = bu[:, n:2 * n]
    ar = abr_ref[...]
    ai = abi_ref[...]

    def step(t, carry):
        hr, hi = carry
        r0 = pl.multiple_of(t * batch, batch)
        nr = ar * hr - ai * hi + hre_ref[pl.ds(r0, batch), :]
        ni = ar * hi + ai * hr + him_ref[pl.ds(r0, batch), :]
        hre_ref[pl.ds(r0, batch), :] = nr
        him_ref[pl.ds(r0, batch), :] = ni
        return nr, ni

    hr, hi = lax.fori_loop(0, tl, step, (cre_ref[...], cim_ref[...]), unroll=4)
    cre_ref[...] = hr
    cim_ref[...] = hi

    y = (_dot(hre_ref[...].astype(BF16), cc_ref[0:n, :])
         + _dot(him_ref[...].astype(BF16), cc_ref[n:2 * n, :])
         + d_ref[...] * u)
    y = _gelu(y)
    y = y * _sigmoid(_dot(y.astype(BF16), gw_ref[...]) + gb_ref[...])
    y_ref[...] = (_rms_scale(y) * gmix_ref[...]).astype(BF16)


def _s5(u_tm, abr, abi, bb, cc, dskip, gw, gb, gmix_c, batch):
    rows, w = u_tm.shape
    tl = SSM_TL
    tr = tl * batch
    n = SSM_NSTATE
    kern = functools.partial(_s5_kernel, batch=batch, tl=tl)
    return pl.pallas_call(
        kern,
        grid=(rows // tr,),
        in_specs=[pl.BlockSpec((tr, w), lambda i: (i, 0)),
                  _resident(abr.shape), _resident(abi.shape), _resident(bb.shape),
                  _resident(cc.shape), _resident(dskip.shape), _resident(gw.shape),
                  _resident(gb.shape), _resident(gmix_c.shape)],
        out_specs=pl.BlockSpec((tr, w), lambda i: (i, 0)),
        out_shape=jax.ShapeDtypeStruct((rows, w), BF16),
        scratch_shapes=[pltpu.VMEM((tr, n), F32), pltpu.VMEM((tr, n), F32),
                        pltpu.VMEM((batch, n), F32), pltpu.VMEM((batch, n), F32)],
        compiler_params=pltpu.CompilerParams(
            dimension_semantics=("arbitrary",), vmem_limit_bytes=VMEM_LIMIT),
        name="s5",
    )(u_tm, abr, abi, bb, cc, dskip, gw, gb, gmix_c)


def _out_ffn_kernel(x_ref, ya_ref, o_ref, yc_ref, gmix_b_ref, wa_ref, wb_ref, wc_ref,
                    g1_ref, b1_ref, wg_ref, wu_ref, wd_ref, g2_ref, b2_ref, out_ref,
                    *, alpha, f_chunk):
    x = x_ref[...]
    yb = (_rms_scale(o_ref[...]) * gmix_b_ref[...]).astype(BF16)
    y = _dot(ya_ref[...], wa_ref[...]) + _dot(yb, wb_ref[...]) + _dot(yc_ref[...], wc_ref[...])
    x1 = _layer_norm(alpha * x + y, g1_ref[...], b1_ref[...])
    out_ref[...] = _swiglu_ln(x1, wg_ref, wu_ref, wd_ref, g2_ref[...], b2_ref[...], alpha, f_chunk)


def _out_ffn(x, ya, o, yc_tm, gmix_b, wa, wb, wc, g1, b1, wg, wu, wd, g2, b2, alpha):
    t, d = x.shape
    tm = ROW_TILE
    spb = yc_tm.shape[0] // tm
    row = lambda w: pl.BlockSpec((tm, w), lambda i: (i, 0))
    kern = functools.partial(_out_ffn_kernel, alpha=alpha, f_chunk=_ffn_chunk(wg.shape[1]))
    return pl.pallas_call(
        kern,
        grid=(t // tm,),
        in_specs=[row(d), row(GM_WIDTH), row(MLA_WIDTH),
                  pl.BlockSpec((tm, SSM_WIDTH), lambda i: (i % spb, i // spb)),
                  _resident(gmix_b.shape), _resident(wa.shape), _resident(wb.shape),
                  _resident(wc.shape), _resident(g1.shape), _resident(b1.shape),
                  _resident(wg.shape), _resident(wu.shape), _resident(wd.shape),
                  _resident(g2.shape), _resident(b2.shape)],
        out_specs=row(d),
        out_shape=jax.ShapeDtypeStruct((t, d), F32),
        compiler_params=pltpu.CompilerParams(
            dimension_semantics=("arbitrary",), vmem_limit_bytes=VMEM_LIMIT),
        name="out_ffn",
    )(x, ya, o, yc_tm, gmix_b, wa, wb, wc, g1, b1, wg, wu, wd, g2, b2)


def _pack_w_in(w_in):
    d = w_in.shape[0]
    o1 = 2 * GM_WIDTH
    o2 = o1 + Q_LORA
    o3 = o2 + KV_LORA
    o4 = o3 + MLA_ROPE
    kr_tile = jnp.zeros((d, HEAD_PAD), w_in.dtype).at[:, MLA_NOPE:MLA_NOPE + MLA_ROPE].set(w_in[:, o3:o4])
    return jnp.concatenate([w_in[:, :o3], kr_tile, w_in[:, o4:]], axis=1).astype(BF16)


def _pack_w_uq(w_uq):
    r = w_uq.shape[0]
    w = w_uq.reshape(r, MLA_HEADS, MLA_NOPE + MLA_ROPE)
    w = jnp.pad(w, ((0, 0), (0, 0), (0, HEAD_PAD - MLA_NOPE - MLA_ROPE)))
    return w.reshape(r, MLA_HEADS * HEAD_PAD).astype(BF16)


def _pack_w_ukv(w_ukv):
    r = w_ukv.shape[0]
    w = w_ukv.reshape(r, MLA_HEADS, MLA_NOPE + MLA_V)
    wk = jnp.pad(w[:, :, :MLA_NOPE], ((0, 0), (0, 0), (0, HEAD_PAD - MLA_NOPE)))
    wvt = jnp.pad(jnp.transpose(w[:, :, MLA_NOPE:], (1, 2, 0)), ((0, 0), (0, MLA_VP - MLA_V), (0, 0)))
    return (wk.reshape(r, MLA_HEADS * HEAD_PAD).astype(BF16),
            wvt.reshape(MLA_HEADS * MLA_VP, r).astype(BF16))


def _pack_ssm_c(c_re, c_im):
    def blockdiag(cm):
        eye = jnp.eye(SSM_GROUPS, dtype=cm.dtype)
        m = jnp.transpose(cm, (0, 2, 1))[:, :, None, :] * eye[:, None, :, None]
        return m.reshape(SSM_NSTATE, SSM_WIDTH)
    return jnp.concatenate([blockdiag(c_re), -blockdiag(c_im)], axis=0).astype(BF16)


def kernel(x, positions, ln_g, ln_b, ffn1_w_gate, ffn1_w_up, ffn1_w_down, w_in, gmlp_norm_g, gmlp_ws, gmlp_bs, mla_q_norm_g, mla_w_uq, mla_kv_norm_g, mla_w_ukv, ssm_a_re, ssm_a_im, ssm_b_re, ssm_b_im, ssm_c_re, ssm_c_im, ssm_d, ssm_log_dt, ssm_glu_w, ssm_glu_b, mix_norm_g, w_out, ffn2_w_gate, ffn2_w_up, ffn2_w_down):
    batch, seq, d = x.shape
    depth = w_in.shape[0]
    t = batch * seq
    assert batch == 8, "the S5 kernel keeps the batch on the 8 sublanes of a vreg"
    assert seq % max(ROW_TILE, ATTN_TQ, SSM_TL) == 0 and ROW_TILE % GM_CHUNK == 0
    assert ATTN_TQ == ATTN_TK == ROW_TILE
    alpha = (2 * depth) ** 0.25

    cos_t, sin_t = _rope_tables(positions)
    xf = x.reshape(t, d)
    row = lambda a: a.reshape(1, -1)
    for l in range(depth):
        xf = _ffn_ln(xf, ffn1_w_gate[l].astype(BF16), ffn1_w_up[l].astype(BF16),
                     ffn1_w_down[l].astype(BF16), row(ln_g[l, 0]), row(ln_b[l, 0]), alpha)

        wk, wvt = _pack_w_ukv(mla_w_ukv[l])
        gbias = jnp.repeat(gmlp_bs[l].T, GM_HEAD_DIM, axis=1)
        gmix = mix_norm_g[l]
        ya, q, k, vt, u = _in_proj(
            xf, _pack_w_in(w_in[l]), row(gmlp_norm_g[l]), gmlp_ws[l], gbias,
            row(mla_q_norm_g[l]), _pack_w_uq(mla_w_uq[l]), row(mla_kv_norm_g[l]), wk, wvt,
            cos_t, sin_t, row(gmix[:GM_WIDTH]), batch, seq)

        o = _attention(q, k, vt, batch, seq)

        abr, abi, bb = _s5_prep(ssm_a_re[l], ssm_a_im[l], ssm_log_dt[l], ssm_b_re[l], ssm_b_im[l], batch)
        yc_tm = _s5(u.reshape(t, SSM_WIDTH), abr, abi, bb, _pack_ssm_c(ssm_c_re[l], ssm_c_im[l]),
                    row(ssm_d[l]), ssm_glu_w[l].astype(BF16), row(ssm_glu_b[l]),
                    row(gmix[GM_WIDTH + MLA_WIDTH:]), batch)

        wo = w_out[l].astype(BF16)
        xf = _out_ffn(xf, ya, o, yc_tm.reshape(seq, batch * SSM_WIDTH),
                      row(gmix[GM_WIDTH:GM_WIDTH + MLA_WIDTH]),
                      wo[:GM_WIDTH], wo[GM_WIDTH:GM_WIDTH + MLA_WIDTH], wo[GM_WIDTH + MLA_WIDTH:],
                      row(ln_g[l, 1]), row(ln_b[l, 1]),
                      ffn2_w_gate[l].astype(BF16), ffn2_w_up[l].astype(BF16),
                      ffn2_w_down[l].astype(BF16), row(ln_g[l, 2]), row(ln_b[l, 2]), alpha)
    return xf.reshape(batch, seq, d)
```

```python
import functools
import math

import jax
import jax.numpy as jnp
from jax import lax
from jax.experimental import pallas as pl
from jax.experimental.pallas import tpu as pltpu

F32 = jnp.float32
BF16 = jnp.bfloat16

GM_HEADS = 4
GM_HEAD_DIM = 64
GM_WIDTH = GM_HEADS * GM_HEAD_DIM
GM_CHUNK = 128
MLA_HEADS = 8
MLA_NOPE = 64
MLA_ROPE = 32
MLA_V = 64
MLA_WIDTH = MLA_HEADS * MLA_V
Q_LORA = 256
KV_LORA = 128
ROPE_BASE = 10000.0
SSM_GROUPS = 16
SSM_GROUP_CH = 16
SSM_WIDTH = SSM_GROUPS * SSM_GROUP_CH
SSM_STATE = 64
SSM_NSTATE = SSM_GROUPS * SSM_STATE
LN_EPS = 1e-5
RMS_EPS = 1e-6
NEG_BIG = -1e30

LANES = 128
BF16_SUBLANES = 16
HEAD_PAD = LANES
MLA_VP = MLA_V + BF16_SUBLANES
SOFTMAX_C = (MLA_NOPE + MLA_ROPE) ** -0.5 * math.log2(math.e)
VMEM_LIMIT = 56 * 1024 * 1024

ROW_TILE = 512
ATTN_TQ = 512
ATTN_TK = 512
ATTN_CHUNK = ATTN_TK
SSM_TL = 64


def _gelu(x):
    c = math.sqrt(2.0 / math.pi)
    return 0.5 * x * (1.0 + jnp.tanh(c * (x + 0.044715 * (x * x * x))))


def _sigmoid(x):
    return 1.0 / (1.0 + jnp.exp(-x))


def _layer_norm(r, g, b):
    mu = jnp.mean(r, axis=-1, keepdims=True)
    d = r - mu
    var = jnp.mean(d * d, axis=-1, keepdims=True)
    return d * lax.rsqrt(var + LN_EPS) * g + b


def _rms_scale(y):
    return y * lax.rsqrt(jnp.mean(y * y, axis=-1, keepdims=True) + RMS_EPS)


def _dot(a, b):
    return jnp.dot(a, b, preferred_element_type=F32)


def _swiglu_ln(x, wg_ref, wu_ref, wd_ref, g, b, alpha, f_chunk):
    xb = x.astype(BF16)
    d_ff = wg_ref.shape[1]
    y = None
    for c0 in range(0, d_ff, f_chunk):
        hg = _dot(xb, wg_ref[:, c0:c0 + f_chunk])
        hu = _dot(xb, wu_ref[:, c0:c0 + f_chunk])
        act = (hg * _sigmoid(hg)) * hu
        part = _dot(act.astype(BF16), wd_ref[c0:c0 + f_chunk, :])
        y = part if y is None else y + part
    return _layer_norm(alpha * x + 0.5 * y, g, b)


def _ffn_ln_kernel(x_ref, wg_ref, wu_ref, wd_ref, g_ref, b_ref, o_ref, *, alpha, f_chunk):
    o_ref[...] = _swiglu_ln(x_ref[...], wg_ref, wu_ref, wd_ref, g_ref[...], b_ref[...], alpha, f_chunk)


def _resident(shape):
    nd = len(shape)
    return pl.BlockSpec(shape, lambda *_: (0,) * nd, pipeline_mode=pl.Buffered(1))


def _ffn_chunk(d_ff):
    return d_ff // 2 if (d_ff // 2) % LANES == 0 else d_ff


def _ffn_ln(x, wg, wu, wd, g, b, alpha):
    t, d = x.shape
    tm = ROW_TILE
    kern = functools.partial(_ffn_ln_kernel, alpha=alpha, f_chunk=_ffn_chunk(wg.shape[1]))
    return pl.pallas_call(
        kern,
        grid=(t // tm,),
        in_specs=[
            pl.BlockSpec((tm, d), lambda i: (i, 0)),
            _resident(wg.shape), _resident(wu.shape), _resident(wd.shape),
            _resident(g.shape), _resident(b.shape),
        ],
        out_specs=pl.BlockSpec((tm, d), lambda i: (i, 0)),
        out_shape=jax.ShapeDtypeStruct((t, d), F32),
        compiler_params=pltpu.CompilerParams(
            dimension_semantics=("arbitrary",), vmem_limit_bytes=VMEM_LIMIT),
        name="ffn_ln",
    )(x, wg, wu, wd, g, b)


def _rope_table_kernel(pos_ref, freq_ref, cos_ref, sin_ref):
    ang = pos_ref[...].astype(F32) * freq_ref[...]
    lane = lax.broadcasted_iota(jnp.int32, ang.shape, 1)
    half = MLA_ROPE // 2
    c = jnp.cos(ang)
    s = jnp.sin(ang)
    in_rope = (lane >= MLA_NOPE) & (lane < MLA_NOPE + MLA_ROPE)
    cos_ref[...] = jnp.where(lane < MLA_NOPE, 1.0, jnp.where(in_rope, c, 0.0))
    sin_ref[...] = jnp.where(in_rope, jnp.where(lane < MLA_NOPE + half, -s, s), 0.0)


def _rope_tables(positions):
    b, s = positions.shape
    t = b * s
    tm = ROW_TILE
    half = MLA_ROPE // 2
    inv_freq = 1.0 / (ROPE_BASE ** (jnp.arange(0, MLA_ROPE, 2, dtype=F32) / MLA_ROPE))
    freq_row = jnp.zeros((1, HEAD_PAD), F32)
    freq_row = freq_row.at[0, MLA_NOPE:MLA_NOPE + half].set(inv_freq)
    freq_row = freq_row.at[0, MLA_NOPE + half:MLA_NOPE + MLA_ROPE].set(inv_freq)
    pos = positions.reshape(t, 1)
    return pl.pallas_call(
        _rope_table_kernel,
        grid=(t // tm,),
        in_specs=[pl.BlockSpec((tm, 1), lambda i: (i, 0)),
                  pl.BlockSpec((1, HEAD_PAD), lambda i: (0, 0))],
        out_specs=[pl.BlockSpec((tm, HEAD_PAD), lambda i: (i, 0))] * 2,
        out_shape=[jax.ShapeDtypeStruct((t, HEAD_PAD), F32)] * 2,
        compiler_params=pltpu.CompilerParams(dimension_semantics=("arbitrary",)),
        name="rope_tables",
    )(pos, freq_row)


def _rope_tile(x, cos_t, sin_t, lane):
    half = MLA_ROPE // 2
    partner = jnp.where(lane < MLA_NOPE + half,
                        pltpu.roll(x, HEAD_PAD - half, 1),
                        pltpu.roll(x, half, 1))
    return x * cos_t + partner * sin_t


def _in_proj_kernel(x_ref, win_ref, gng_ref, gws_ref, gbias_ref, qg_ref, wuq_ref, kvg_ref,
                    wuk_ref, wuvt_ref, vones_ref, cos_ref, sin_ref, gmix_a_ref,
                    ya_ref, q_ref, k_ref, vt_ref, u_ref):
    tm = x_ref.shape[0]
    xb = x_ref[...].astype(BF16)
    h = _dot(xb, win_ref[...])
    o_v = GM_WIDTH
    o_q = 2 * GM_WIDTH
    o_kv = o_q + Q_LORA
    o_kr = o_kv + KV_LORA
    o_ssm = o_kr + HEAD_PAD

    u_ref[...] = h[:, o_ssm:o_ssm + SSM_WIDTH]

    ug = _gelu(h[:, 0:GM_WIDTH])
    vg = _gelu(h[:, o_v:o_v + GM_WIDTH])
    lane_g = lax.broadcasted_iota(jnp.int32, (1, GM_WIDTH), 1)
    head_masks = [(lane_g >= hd * GM_HEAD_DIM) & (lane_g < (hd + 1) * GM_HEAD_DIM)
                  for hd in range(GM_HEADS)]

    def seg_mean(a):
        out = jnp.zeros_like(a)
        for m in head_masks:
            s = jnp.sum(jnp.where(m, a, 0.0), axis=-1, keepdims=True) * (1.0 / GM_HEAD_DIM)
            out = jnp.where(m, s, out)
        return out

    dv = vg - seg_mean(vg)
    vn = dv * lax.rsqrt(seg_mean(dv * dv) + LN_EPS) * gng_ref[...]
    vnb = vn.astype(BF16)
    row = lax.broadcasted_iota(jnp.int32, (GM_CHUNK, GM_CHUNK), 0)
    col = lax.broadcasted_iota(jnp.int32, (GM_CHUNK, GM_CHUNK), 1)
    tril = col <= row
    wc = [jnp.where(tril, gws_ref[hd], 0.0).astype(BF16) for hd in range(GM_HEADS)]
    gbias = gbias_ref[...]
    gmix_a = gmix_a_ref[...]
    for c0 in range(0, tm, GM_CHUNK):
        vc = vnb[c0:c0 + GM_CHUNK, :]
        z = jnp.zeros((GM_CHUNK, GM_WIDTH), F32)
        for hd in range(GM_HEADS):
            z = jnp.where(head_masks[hd], _dot(wc[hd], vc), z)
        ya = ug[c0:c0 + GM_CHUNK, :] * (z + gbias)
        ya_ref[c0:c0 + GM_CHUNK, :] = (_rms_scale(ya) * gmix_a).astype(BF16)

    cos_t = cos_ref[...]
    sin_t = sin_ref[...]
    lane = lax.broadcasted_iota(jnp.int32, (tm, HEAD_PAD), 1)
    cq = h[:, o_q:o_q + Q_LORA]
    cqn = (_rms_scale(cq) * qg_ref[...]).astype(BF16)
    q = _dot(cqn, wuq_ref[...])
    for hd in range(MLA_HEADS):
        qh = q[:, hd * HEAD_PAD:(hd + 1) * HEAD_PAD]
        q_ref[:, hd * HEAD_PAD:(hd + 1) * HEAD_PAD] = (
            _rope_tile(qh, cos_t, sin_t, lane) * SOFTMAX_C).astype(BF16)
    ckv = h[:, o_kv:o_kv + KV_LORA]
    ckvn = (_rms_scale(ckv) * kvg_ref[...]).astype(BF16)
    kn = _dot(ckvn, wuk_ref[...])
    kr = _rope_tile(h[:, o_kr:o_kr + HEAD_PAD], cos_t, sin_t, lane)
    for hd in range(MLA_HEADS):
        k_ref[:, hd * HEAD_PAD:(hd + 1) * HEAD_PAD] = (kn[:, hd * HEAD_PAD:(hd + 1) * HEAD_PAD] + kr).astype(BF16)
    vt = lax.dot_general(wuvt_ref[...], ckvn, (((1,), (1,)), ((), ())),
                         preferred_element_type=F32)
    vt_ref[0, 0] = (vt + vones_ref[...]).astype(BF16)


def _in_proj(x, win, gng, gws, gbias, qg, wuq, kvg, wuk, wuvt, cos_t, sin_t, gmix_a, batch, seq):
    t, d = x.shape
    tm = ROW_TILE
    spb = seq // tm
    row = lambda w: pl.BlockSpec((tm, w), lambda i: (i, 0))
    vrows = MLA_HEADS * MLA_VP
    is_one = (jnp.arange(vrows) % MLA_VP) == MLA_V
    vones = jnp.broadcast_to(is_one[:, None], (vrows, tm)).astype(F32)
    return pl.pallas_call(
        _in_proj_kernel,
        grid=(t // tm,),
        in_specs=[row(d), _resident(win.shape), _resident(gng.shape), _resident(gws.shape),
                  _resident(gbias.shape), _resident(qg.shape), _resident(wuq.shape),
                  _resident(kvg.shape), _resident(wuk.shape), _resident(wuvt.shape),
                  _resident(vones.shape), row(HEAD_PAD), row(HEAD_PAD), _resident(gmix_a.shape)],
        out_specs=[row(GM_WIDTH), row(MLA_HEADS * HEAD_PAD), row(MLA_HEADS * HEAD_PAD),
                   pl.BlockSpec((1, 1, vrows, tm), lambda i: (i // spb, i % spb, 0, 0)),
                   pl.BlockSpec((tm, SSM_WIDTH), lambda i: (i % spb, i // spb))],
        out_shape=[jax.ShapeDtypeStruct((t, GM_WIDTH), BF16),
                   jax.ShapeDtypeStruct((t, MLA_HEADS * HEAD_PAD), BF16),
                   jax.ShapeDtypeStruct((t, MLA_HEADS * HEAD_PAD), BF16),
                   jax.ShapeDtypeStruct((batch, spb, vrows, tm), BF16),
                   jax.ShapeDtypeStruct((seq, batch * SSM_WIDTH), F32)],
        compiler_params=pltpu.CompilerParams(
            dimension_semantics=("arbitrary",), vmem_limit_bytes=VMEM_LIMIT),
        name="in_proj",
    )(x, win, gng, gws, gbias, qg, wuq, kvg, wuk, wuvt, vones, cos_t, sin_t, gmix_a)


def _attn_kernel(q_ref, k_ref, vt_ref, o_ref, s0_ref, s1_ref, bm0_ref, bm1_ref, m_ref, acc_ref, *, tq, nq):
    tk = tq

    def stage(score, acc_from):
        masked = False
        score_to = None
        if acc_from is not None:
            ja, (sa_ref, bma_ref) = acc_from
            m_old = [m_ref[hh] for hh in range(2)]
            m_new = [jnp.maximum(m_old[hh], bma_ref[hh]) for hh in range(2)]
        if score is not None:
            qb_s, js, (ss_ref, bms_ref), masked = score
            score_to = score
            q0 = pl.multiple_of(qb_s * tq, tq)
            qs = [q_ref[0, pl.ds(q0, tq), hh * HEAD_PAD:(hh + 1) * HEAD_PAD] for hh in range(2)]
            start = pl.multiple_of(js * tk, tk)
            bmax = [None, None]
        order = [(kind, c0) for c0 in range(0, tk, ATTN_CHUNK) for kind in ("s", "a")]
        for kind, c0 in order:
            for hh in range(2):
                rows = slice(hh * MLA_VP, (hh + 1) * MLA_VP)
                if kind == "s" and score_to is not None:
                    kb = k_ref[0, pl.ds(start + c0, ATTN_CHUNK), hh * HEAD_PAD:(hh + 1) * HEAD_PAD]
                    s = lax.dot_general(kb, qs[hh], (((1,), (1,)), ((), ())),
                                        preferred_element_type=F32)
                    if masked:
                        krow = lax.broadcasted_iota(jnp.int32, s.shape, 0) + c0
                        qcol = lax.broadcasted_iota(jnp.int32, s.shape, 1)
                        s = jnp.where(krow <= qcol, s, NEG_BIG)
                    ss_ref[hh, c0:c0 + ATTN_CHUNK, :] = s
                    cmax = jnp.max(s, axis=0, keepdims=True)
                    bmax[hh] = cmax if bmax[hh] is None else jnp.maximum(bmax[hh], cmax)
                if kind == "a" and acc_from is not None:
                    p = jnp.exp2(sa_ref[hh, c0:c0 + ATTN_CHUNK, :] - m_new[hh]).astype(BF16)
                    d = _dot(vt_ref[0, ja, rows, c0:c0 + ATTN_CHUNK], p)
                    if c0 == 0:
                        acc_ref[rows, :] = jnp.exp2(m_old[hh] - m_new[hh]) * acc_ref[rows, :] + d
                    else:
                        acc_ref[rows, :] += d
        for hh in range(2):
            if acc_from is not None:
                m_ref[hh] = m_new[hh]
            if score_to is not None:
                bms_ref[hh] = bmax[hh]

    bufs = ((s0_ref, bm0_ref), (s1_ref, bm1_ref))

    def reset():
        m_ref[...] = jnp.full(m_ref.shape, NEG_BIG, F32)
        acc_ref[...] = jnp.zeros(acc_ref.shape, F32)

    def finalize(qb):
        outs = []
        for hh in range(2):
            r0 = hh * MLA_VP
            inv_l = 1.0 / acc_ref[r0 + MLA_V:r0 + MLA_V + 1, :]
            outs.append(acc_ref[r0:r0 + MLA_V, :] * inv_l)
        o_ref[0, pl.ds(pl.multiple_of(qb * tq, tq), tq), :] = jnp.concatenate(outs, axis=0).T

    diag_buf = (0, 1, 1, 0)
    reset()
    stage((0, 0, bufs[0], True), None)

    def four_q_blocks(g, carry):
        for r in range(4):
            qb = 4 * g + r
            x = diag_buf[r]

            def pair(t, c, qb=qb, x=x):
                stage((qb, 2 * t, bufs[x ^ 1], False), (jnp.where(t == 0, qb, 2 * t - 1), bufs[x]))
                stage((qb, 2 * t + 1, bufs[x], False), (2 * t, bufs[x ^ 1]))
                return c

            lax.fori_loop(0, qb // 2, pair, 0)
            if r % 2 == 1:
                stage((qb, qb - 1, bufs[x ^ 1], False), (jnp.where(qb == 1, qb, qb - 2), bufs[x]))
                last_buf = x ^ 1
            else:
                last_buf = x
            assert diag_buf[(r + 1) % 4] == last_buf ^ 1
            stage((jnp.minimum(qb + 1, nq - 1), jnp.minimum(qb + 1, nq - 1), bufs[last_buf ^ 1], True),
                  (jnp.where(qb == 0, qb, qb - 1), bufs[last_buf]))
            finalize(qb)
            reset()
        return carry

    lax.fori_loop(0, nq // 4, four_q_blocks, 0)


def _attention(q, k, vt, batch, seq):
    tq, tk = ATTN_TQ, ATTN_TK
    nq = seq // tq
    assert nq % 4 == 0
    q3 = q.reshape(batch, seq, MLA_HEADS * HEAD_PAD)
    k3 = k.reshape(batch, seq, MLA_HEADS * HEAD_PAD)
    kern = functools.partial(_attn_kernel, tq=tq, nq=nq)
    o = pl.pallas_call(
        kern,
        grid=(batch, MLA_HEADS // 2),
        in_specs=[pl.BlockSpec((1, seq, 2 * HEAD_PAD), lambda b, h: (b, 0, h)),
                  pl.BlockSpec((1, seq, 2 * HEAD_PAD), lambda b, h: (b, 0, h)),
                  pl.BlockSpec((1, seq // tk, 2 * MLA_VP, tk), lambda b, h: (b, 0, h, 0))],
        out_specs=pl.BlockSpec((1, seq, 2 * MLA_V), lambda b, h: (b, 0, h)),
        out_shape=jax.ShapeDtypeStruct((batch, seq, MLA_WIDTH), F32),
        scratch_shapes=[pltpu.VMEM((2, tk, tq), F32), pltpu.VMEM((2, tk, tq), F32),
                        pltpu.VMEM((2, 1, tq), F32), pltpu.VMEM((2, 1, tq), F32),
                        pltpu.VMEM((2, 1, tq), F32), pltpu.VMEM((2 * MLA_VP, tq), F32)],
        compiler_params=pltpu.CompilerParams(
            dimension_semantics=("arbitrary", "arbitrary"), vmem_limit_bytes=VMEM_LIMIT),
        name="attention",
    )(q3, k3, vt)
    return o.reshape(batch * seq, MLA_WIDTH)


def _s5_prep_kernel(ar_ref, ai_ref, dt_ref, bre_ref, bim_ref, abr_ref, abi_ref, bb_ref):
    ar = ar_ref[...]
    ai = ai_ref[...]
    dt = jnp.exp(dt_ref[...])
    mag = jnp.exp(ar * dt)
    abr = mag * jnp.cos(ai * dt)
    abi = mag * jnp.sin(ai * dt)
    den = ar * ar + ai * ai
    cr = ((abr - 1.0) * ar + abi * ai) / den
    ci = (abi * ar - (abr - 1.0) * ai) / den
    abr_ref[...] = jnp.broadcast_to(abr, abr_ref.shape)
    abi_ref[...] = jnp.broadcast_to(abi, abi_ref.shape)
    bre = bre_ref[...]
    bim = bim_ref[...]
    r = lax.broadcasted_iota(jnp.int32, bre.shape, 0) // SSM_GROUP_CH
    c = lax.broadcasted_iota(jnp.int32, bre.shape, 1) // SSM_STATE
    same = r == c
    bb_ref[:, 0:SSM_NSTATE] = jnp.where(same, cr * bre - ci * bim, 0.0).astype(BF16)
    bb_ref[:, SSM_NSTATE:2 * SSM_NSTATE] = jnp.where(same, cr * bim + ci * bre, 0.0).astype(BF16)


def _s5_prep(a_re, a_im, log_dt, b_re, b_im, batch):
    n = SSM_NSTATE
    ar = a_re.reshape(1, n)
    ai = a_im.reshape(1, n)
    dt = jnp.repeat(log_dt, SSM_STATE).reshape(1, n)
    def expand(bm):
        cols = jnp.transpose(bm, (2, 0, 1)).reshape(SSM_GROUP_CH, n)
        return jnp.tile(cols, (SSM_GROUPS, 1))
    whole = lambda shp: pl.BlockSpec(shp, lambda: (0,) * len(shp))
    return pl.pallas_call(
        _s5_prep_kernel,
        in_specs=[whole((1, n)), whole((1, n)), whole((1, n)),
                  whole((SSM_WIDTH, n)), whole((SSM_WIDTH, n))],
        out_specs=[whole((batch, n)), whole((batch, n)), whole((SSM_WIDTH, 2 * n))],
        out_shape=[jax.ShapeDtypeStruct((batch, n), F32), jax.ShapeDtypeStruct((batch, n), F32),
                   jax.ShapeDtypeStruct((SSM_WIDTH, 2 * n), BF16)],
        name="s5_prep",
    )(ar, ai, dt, expand(b_re), expand(b_im))


def _s5_kernel(u_ref, abr_ref, abi_ref, bb_ref, cc_ref, d_ref, gw_ref, gb_ref, gmix_ref,
               y_ref, hre_ref, him_ref, cre_ref, cim_ref, *, batch, tl):
    n = SSM_NSTATE

    @pl.when(pl.program_id(0) == 0)
    def _():
        cre_ref[...] = jnp.zeros(cre_ref.shape, F32)
        cim_ref[...] = jnp.zeros(cim_ref.shape, F32)

    u = u_ref[...]
    bu = _dot(u.astype(BF16), bb_ref[...])
    hre_ref[...] = bu[:, 0:n]
    him_ref[...] = bu[:, n:2 * n]
    ar = abr_ref[...]
    ai = abi_ref[...]

    def step(t, carry):
        hr, hi = carry
        r0 = pl.multiple_of(t * batch, batch)
        nr = ar * hr - ai * hi + hre_ref[pl.ds(r0, batch), :]
        ni = ar * hi + ai * hr + him_ref[pl.ds(r0, batch), :]
        hre_ref[pl.ds(r0, batch), :] = nr
        him_ref[pl.ds(r0, batch), :] = ni
        return nr, ni

    hr, hi = lax.fori_loop(0, tl, step, (cre_ref[...], cim_ref[...]), unroll=4)
    cre_ref[...] = hr
    cim_ref[...] = hi

    y = (_dot(hre_ref[...].astype(BF16), cc_ref[0:n, :])
         + _dot(him_ref[...].astype(BF16), cc_ref[n:2 * n, :])
         + d_ref[...] * u)
    y = _gelu(y)
    y = y * _sigmoid(_dot(y.astype(BF16), gw_ref[...]) + gb_ref[...])
    y_ref[...] = (_rms_scale(y) * gmix_ref[...]).astype(BF16)


def _s5(u_tm, abr, abi, bb, cc, dskip, gw, gb, gmix_c, batch):
    rows, w = u_tm.shape
    tl = SSM_TL
    tr = tl * batch
    n = SSM_NSTATE
    kern = functools.partial(_s5_kernel, batch=batch, tl=tl)
    return pl.pallas_call(
        kern,
        grid=(rows // tr,),
        in_specs=[pl.BlockSpec((tr, w), lambda i: (i, 0)),
                  _resident(abr.shape), _resident(abi.shape), _resident(bb.shape),
                  _resident(cc.shape), _resident(dskip.shape), _resident(gw.shape),
                  _resident(gb.shape), _resident(gmix_c.shape)],
        out_specs=pl.BlockSpec((tr, w), lambda i: (i, 0)),
        out_shape=jax.ShapeDtypeStruct((rows, w), BF16),
        scratch_shapes=[pltpu.VMEM((tr, n), F32), pltpu.VMEM((tr, n), F32),
                        pltpu.VMEM((batch, n), F32), pltpu.VMEM((batch, n), F32)],
        compiler_params=pltpu.CompilerParams(
            dimension_semantics=("arbitrary",), vmem_limit_bytes=VMEM_LIMIT),
        name="s5",
    )(u_tm, abr, abi, bb, cc, dskip, gw, gb, gmix_c)


def _out_ffn_kernel(x_ref, ya_ref, o_ref, yc_ref, gmix_b_ref, wa_ref, wb_ref, wc_ref,
                    g1_ref, b1_ref, wg_ref, wu_ref, wd_ref, g2_ref, b2_ref, out_ref,
                    *, alpha, f_chunk):
    x = x_ref[...]
    yb = (_rms_scale(o_ref[...]) * gmix_b_ref[...]).astype(BF16)
    y = _dot(ya_ref[...], wa_ref[...]) + _dot(yb, wb_ref[...]) + _dot(yc_ref[...], wc_ref[...])
    x1 = _layer_norm(alpha * x + y, g1_ref[...], b1_ref[...])
    out_ref[...] = _swiglu_ln(x1, wg_ref, wu_ref, wd_ref, g2_ref[...], b2_ref[...], alpha, f_chunk)


def _out_ffn(x, ya, o, yc_tm, gmix_b, wa, wb, wc, g1, b1, wg, wu, wd, g2, b2, alpha):
    t, d = x.shape
    tm = ROW_TILE
    spb = yc_tm.shape[0] // tm
    row = lambda w: pl.BlockSpec((tm, w), lambda i: (i, 0))
    kern = functools.partial(_out_ffn_kernel, alpha=alpha, f_chunk=_ffn_chunk(wg.shape[1]))
    return pl.pallas_call(
        kern,
        grid=(t // tm,),
        in_specs=[row(d), row(GM_WIDTH), row(MLA_WIDTH),
                  pl.BlockSpec((tm, SSM_WIDTH), lambda i: (i % spb, i // spb)),
                  _resident(gmix_b.shape), _resident(wa.shape), _resident(wb.shape),
                  _resident(wc.shape), _resident(g1.shape), _resident(b1.shape),
                  _resident(wg.shape), _resident(wu.shape), _resident(wd.shape),
                  _resident(g2.shape), _resident(b2.shape)],
        out_specs=row(d),
        out_shape=jax.ShapeDtypeStruct((t, d), F32),
        compiler_params=pltpu.CompilerParams(
            dimension_semantics=("arbitrary",), vmem_limit_bytes=VMEM_LIMIT),
        name="out_ffn",
    )(x, ya, o, yc_tm, gmix_b, wa, wb, wc, g1, b1, wg, wu, wd, g2, b2)


def _pack_w_in(w_in):
    d = w_in.shape[0]
    o1 = 2 * GM_WIDTH
    o2 = o1 + Q_LORA
    o3 = o2 + KV_LORA
    o4 = o3 + MLA_ROPE
    kr_tile = jnp.zeros((d, HEAD_PAD), w_in.dtype).at[:, MLA_NOPE:MLA_NOPE + MLA_ROPE].set(w_in[:, o3:o4])
    return jnp.concatenate([w_in[:, :o3], kr_tile, w_in[:, o4:]], axis=1).astype(BF16)


def _pack_w_uq(w_uq):
    r = w_uq.shape[0]
    w = w_uq.reshape(r, MLA_HEADS, MLA_NOPE + MLA_ROPE)
    w = jnp.pad(w, ((0, 0), (0, 0), (0, HEAD_PAD - MLA_NOPE - MLA_ROPE)))
    return w.reshape(r, MLA_HEADS * HEAD_PAD).astype(BF16)


def _pack_w_ukv(w_ukv):
    r = w_ukv.shape[0]
    w = w_ukv.reshape(r, MLA_HEADS, MLA_NOPE + MLA_V)
    wk = jnp.pad(w[:, :, :MLA_NOPE], ((0, 0), (0, 0), (0, HEAD_PAD - MLA_NOPE)))
    wvt = jnp.pad(jnp.transpose(w[:, :, MLA_NOPE:], (1, 2, 0)), ((0, 0), (0, MLA_VP - MLA_V), (0, 0)))
    return (wk.reshape(r, MLA_HEADS * HEAD_PAD).astype(BF16),
            wvt.reshape(MLA_HEADS * MLA_VP, r).astype(BF16))


def _pack_ssm_c(c_re, c_im):
    def blockdiag(cm):
        eye = jnp.eye(SSM_GROUPS, dtype=cm.dtype)
        m = jnp.transpose(cm, (0, 2, 1))[:, :, None, :] * eye[:, None, :, None]
        return m.reshape(SSM_NSTATE, SSM_WIDTH)
    return jnp.concatenate([blockdiag(c_re), -blockdiag(c_im)], axis=0).astype(BF16)


def kernel(x, positions, ln_g, ln_b, ffn1_w_gate, ffn1_w_up, ffn1_w_down, w_in, gmlp_norm_g, gmlp_ws, gmlp_bs, mla_q_norm_g, mla_w_uq, mla_kv_norm_g, mla_w_ukv, ssm_a_re, ssm_a_im, ssm_b_re, ssm_b_im, ssm_c_re, ssm_c_im, ssm_d, ssm_log_dt, ssm_glu_w, ssm_glu_b, mix_norm_g, w_out, ffn2_w_gate, ffn2_w_up, ffn2_w_down):
    batch, seq, d = x.shape
    depth = w_in.shape[0]
    t = batch * seq
    assert batch == 8, "the S5 kernel keeps the batch on the 8 sublanes of a vreg"
    assert seq % max(ROW_TILE, ATTN_TQ, SSM_TL) == 0 and ROW_TILE % GM_CHUNK == 0
    assert ATTN_TQ == ATTN_TK == ROW_TILE
    alpha = (2 * depth) ** 0.25

    cos_t, sin_t = _rope_tables(positions)
    xf = x.reshape(t, d)
    row = lambda a: a.reshape(1, -1)
    for l in range(depth):
        xf = _ffn_ln(xf, ffn1_w_gate[l].astype(BF16), ffn1_w_up[l].astype(BF16),
                     ffn1_w_down[l].astype(BF16), row(ln_g[l, 0]), row(ln_b[l, 0]), alpha)

        wk, wvt = _pack_w_ukv(mla_w_ukv[l])
        gbias = jnp.repeat(gmlp_bs[l].T, GM_HEAD_DIM, axis=1)
        gmix = mix_norm_g[l]
        ya, q, k, vt, u = _in_proj(
            xf, _pack_w_in(w_in[l]), row(gmlp_norm_g[l]), gmlp_ws[l], gbias,
            row(mla_q_norm_g[l]), _pack_w_uq(mla_w_uq[l]), row(mla_kv_norm_g[l]), wk, wvt,
            cos_t, sin_t, row(gmix[:GM_WIDTH]), batch, seq)

        o = _attention(q, k, vt, batch, seq)

        abr, abi, bb = _s5_prep(ssm_a_re[l], ssm_a_im[l], ssm_log_dt[l], ssm_b_re[l], ssm_b_im[l], batch)
        yc_tm = _s5(u.reshape(t, SSM_WIDTH), abr, abi, bb, _pack_ssm_c(ssm_c_re[l], ssm_c_im[l]),
                    row(ssm_d[l]), ssm_glu_w[l].astype(BF16), row(ssm_glu_b[l]),
                    row(gmix[GM_WIDTH + MLA_WIDTH:]), batch)

        wo = w_out[l].astype(BF16)
        xf = _out_ffn(xf, ya, o, yc_tm.reshape(seq, batch * SSM_WIDTH),
                      row(gmix[GM_WIDTH:GM_WIDTH + MLA_WIDTH]),
                      wo[:GM_WIDTH], wo[GM_WIDTH:GM_WIDTH + MLA_WIDTH], wo[GM_WIDTH + MLA_WIDTH:],
                      row(ln_g[l, 1]), row(ln_b[l, 1]),
                      ffn2_w_gate[l].astype(BF16), ffn2_w_up[l].astype(BF16),
                      ffn2_w_down[l].astype(BF16), row(ln_g[l, 2]), row(ln_b[l, 2]), alpha)
    return xf.reshape(batch, seq, d)
```

```python
import functools
import math

import jax
import jax.numpy as jnp
from jax import lax
from jax.experimental import pallas as pl
from jax.experimental.pallas import tpu as pltpu

F32 = jnp.float32
BF16 = jnp.bfloat16

GM_HEADS = 4
GM_HEAD_DIM = 64
GM_WIDTH = GM_HEADS * GM_HEAD_DIM
GM_CHUNK = 128
MLA_HEADS = 8
MLA_NOPE = 64
MLA_ROPE = 32
MLA_V = 64
MLA_WIDTH = MLA_HEADS * MLA_V
Q_LORA = 256
KV_LORA = 128
ROPE_BASE = 10000.0
SSM_GROUPS = 16
SSM_GROUP_CH = 16
SSM_WIDTH = SSM_GROUPS * SSM_GROUP_CH
SSM_STATE = 64
SSM_NSTATE = SSM_GROUPS * SSM_STATE
LN_EPS = 1e-5
RMS_EPS = 1e-6
NEG_BIG = -1e30

LANES = 128
BF16_SUBLANES = 16
HEAD_PAD = LANES
MLA_VP = MLA_V + BF16_SUBLANES
SOFTMAX_C = (MLA_NOPE + MLA_ROPE) ** -0.5 * math.log2(math.e)
VMEM_LIMIT = 56 * 1024 * 1024

ROW_TILE = 512
ATTN_TQ = 512
ATTN_TK = 512
ATTN_CHUNK = ATTN_TK
SSM_TL = 128
SSM_CHUNK = 32


def _gelu(x):
    c = math.sqrt(2.0 / math.pi)
    return 0.5 * x * (1.0 + jnp.tanh(c * (x + 0.044715 * (x * x * x))))


def _sigmoid(x):
    return 1.0 / (1.0 + jnp.exp(-x))


def _layer_norm(r, g, b):
    mu = jnp.mean(r, axis=-1, keepdims=True)
    d = r - mu
    var = jnp.mean(d * d, axis=-1, keepdims=True)
    return d * lax.rsqrt(var + LN_EPS) * g + b


def _rms_scale(y):
    return y * lax.rsqrt(jnp.mean(y * y, axis=-1, keepdims=True) + RMS_EPS)


def _dot(a, b):
    return jnp.dot(a, b, preferred_element_type=F32)


def _swiglu_ln(x, wg_ref, wu_ref, wd_ref, g, b, alpha, f_chunk):
    xb = x.astype(BF16)
    d_ff = wg_ref.shape[1]
    y = None
    for c0 in range(0, d_ff, f_chunk):
        hg = _dot(xb, wg_ref[:, c0:c0 + f_chunk])
        hu = _dot(xb, wu_ref[:, c0:c0 + f_chunk])
        act = (hg * _sigmoid(hg)) * hu
        part = _dot(act.astype(BF16), wd_ref[c0:c0 + f_chunk, :])
        y = part if y is None else y + part
    return _layer_norm(alpha * x + y, g, b)


def _ffn_ln_kernel(x_ref, wg_ref, wu_ref, wd_ref, g_ref, b_ref, o_ref, *, alpha, f_chunk):
    o_ref[...] = _swiglu_ln(x_ref[...], wg_ref, wu_ref, wd_ref, g_ref[...], b_ref[...], alpha, f_chunk)


def _resident(shape):
    nd = len(shape)
    return pl.BlockSpec(shape, lambda *_: (0,) * nd, pipeline_mode=pl.Buffered(1))


def _ffn_chunk(d_ff):
    return d_ff // 2 if (d_ff // 2) % LANES == 0 else d_ff


def _ffn_ln(x, wg, wu, wd, g, b, alpha):
    t, d = x.shape
    tm = ROW_TILE
    kern = functools.partial(_ffn_ln_kernel, alpha=alpha, f_chunk=_ffn_chunk(wg.shape[1]))
    return pl.pallas_call(
        kern,
        grid=(t // tm,),
        in_specs=[
            pl.BlockSpec((tm, d), lambda i: (i, 0)),
            _resident(wg.shape), _resident(wu.shape), _resident(wd.shape),
            _resident(g.shape), _resident(b.shape),
        ],
        out_specs=pl.BlockSpec((tm, d), lambda i: (i, 0)),
        out_shape=jax.ShapeDtypeStruct((t, d), F32),
        compiler_params=pltpu.CompilerParams(
            dimension_semantics=("arbitrary",), vmem_limit_bytes=VMEM_LIMIT),
        name="ffn_ln",
    )(x, wg, wu, wd, g, b)


def _rope_table_kernel(pos_ref, freq_ref, cos_ref, sin_ref):
    ang = pos_ref[...].astype(F32) * freq_ref[...]
    lane = lax.broadcasted_iota(jnp.int32, ang.shape, 1)
    half = MLA_ROPE // 2
    c = jnp.cos(ang)
    s = jnp.sin(ang)
    in_rope = (lane >= MLA_NOPE) & (lane < MLA_NOPE + MLA_ROPE)
    cos_ref[...] = jnp.where(lane < MLA_NOPE, 1.0, jnp.where(in_rope, c, 0.0))
    sin_ref[...] = jnp.where(in_rope, jnp.where(lane < MLA_NOPE + half, -s, s), 0.0)


def _rope_tables(positions):
    b, s = positions.shape
    t = b * s
    tm = ROW_TILE
    half = MLA_ROPE // 2
    inv_freq = 1.0 / (ROPE_BASE ** (jnp.arange(0, MLA_ROPE, 2, dtype=F32) / MLA_ROPE))
    freq_row = jnp.zeros((1, HEAD_PAD), F32)
    freq_row = freq_row.at[0, MLA_NOPE:MLA_NOPE + half].set(inv_freq)
    freq_row = freq_row.at[0, MLA_NOPE + half:MLA_NOPE + MLA_ROPE].set(inv_freq)
    pos = positions.reshape(t, 1)
    return pl.pallas_call(
        _rope_table_kernel,
        grid=(t // tm,),
        in_specs=[pl.BlockSpec((tm, 1), lambda i: (i, 0)),
                  pl.BlockSpec((1, HEAD_PAD), lambda i: (0, 0))],
        out_specs=[pl.BlockSpec((tm, HEAD_PAD), lambda i: (i, 0))] * 2,
        out_shape=[jax.ShapeDtypeStruct((t, HEAD_PAD), F32)] * 2,
        compiler_params=pltpu.CompilerParams(dimension_semantics=("arbitrary",)),
        name="rope_tables",
    )(pos, freq_row)


def _rope_tile(x, cos_t, sin_t, lane):
    half = MLA_ROPE // 2
    partner = jnp.where(lane < MLA_NOPE + half,
                        pltpu.roll(x, HEAD_PAD - half, 1),
                        pltpu.roll(x, half, 1))
    return x * cos_t + partner * sin_t


def _in_proj_kernel(x_ref, win_ref, gng_ref, gws_ref, gbias_ref, qg_ref, wuq_ref, kvg_ref,
                    wuk_ref, wuvt_ref, vones_ref, cos_ref, sin_ref, gmix_a_ref,
                    ya_ref, q_ref, k_ref, vt_ref, u_ref):
    tm = x_ref.shape[0]
    xb = x_ref[...].astype(BF16)
    h = _dot(xb, win_ref[...])
    o_v = GM_WIDTH
    o_q = 2 * GM_WIDTH
    o_kv = o_q + Q_LORA
    o_kr = o_kv + KV_LORA
    o_ssm = o_kr + HEAD_PAD

    u_ref[...] = h[:, o_ssm:o_ssm + SSM_WIDTH]

    ug = _gelu(h[:, 0:GM_WIDTH])
    vg = _gelu(h[:, o_v:o_v + GM_WIDTH])
    lane_g = lax.broadcasted_iota(jnp.int32, (1, GM_WIDTH), 1)
    head_masks = [(lane_g >= hd * GM_HEAD_DIM) & (lane_g < (hd + 1) * GM_HEAD_DIM)
                  for hd in range(GM_HEADS)]

    def seg_mean(a):
        out = jnp.zeros_like(a)
        for m in head_masks:
            s = jnp.sum(jnp.where(m, a, 0.0), axis=-1, keepdims=True) * (1.0 / GM_HEAD_DIM)
            out = jnp.where(m, s, out)
        return out

    dv = vg - seg_mean(vg)
    vn = dv * lax.rsqrt(seg_mean(dv * dv) + LN_EPS) * gng_ref[...]
    vnb = vn.astype(BF16)
    row = lax.broadcasted_iota(jnp.int32, (GM_CHUNK, GM_CHUNK), 0)
    col = lax.broadcasted_iota(jnp.int32, (GM_CHUNK, GM_CHUNK), 1)
    tril = col <= row
    wc = [jnp.where(tril, gws_ref[hd], 0.0).astype(BF16) for hd in range(GM_HEADS)]
    gbias = gbias_ref[...]
    gmix_a = gmix_a_ref[...]
    for c0 in range(0, tm, GM_CHUNK):
        vc = vnb[c0:c0 + GM_CHUNK, :]
        z = jnp.zeros((GM_CHUNK, GM_WIDTH), F32)
        for hd in range(GM_HEADS):
            z = jnp.where(head_masks[hd], _dot(wc[hd], vc), z)
        ya = ug[c0:c0 + GM_CHUNK, :] * (z + gbias)
        ya_ref[c0:c0 + GM_CHUNK, :] = (_rms_scale(ya) * gmix_a).astype(BF16)

    cos_t = cos_ref[...]
    sin_t = sin_ref[...]
    lane = lax.broadcasted_iota(jnp.int32, (tm, HEAD_PAD), 1)
    cq = h[:, o_q:o_q + Q_LORA]
    cqn = (_rms_scale(cq) * qg_ref[...]).astype(BF16)
    q = _dot(cqn, wuq_ref[...])
    for hd in range(MLA_HEADS):
        qh = q[:, hd * HEAD_PAD:(hd + 1) * HEAD_PAD]
        q_ref[:, hd * HEAD_PAD:(hd + 1) * HEAD_PAD] = (
            _rope_tile(qh, cos_t, sin_t, lane) * SOFTMAX_C).astype(BF16)
    ckv = h[:, o_kv:o_kv + KV_LORA]
    ckvn = (_rms_scale(ckv) * kvg_ref[...]).astype(BF16)
    kn = _dot(ckvn, wuk_ref[...])
    kr = _rope_tile(h[:, o_kr:o_kr + HEAD_PAD], cos_t, sin_t, lane)
    for hd in range(MLA_HEADS):
        k_ref[:, hd * HEAD_PAD:(hd + 1) * HEAD_PAD] = (kn[:, hd * HEAD_PAD:(hd + 1) * HEAD_PAD] + kr).astype(BF16)
    vt = lax.dot_general(wuvt_ref[...], ckvn, (((1,), (1,)), ((), ())),
                         preferred_element_type=F32)
    vt_ref[0, 0] = (vt + vones_ref[...]).astype(BF16)


def _in_proj(x, win, gng, gws, gbias, qg, wuq, kvg, wuk, wuvt, cos_t, sin_t, gmix_a, batch, seq):
    t, d = x.shape
    tm = ROW_TILE
    spb = seq // tm
    row = lambda w: pl.BlockSpec((tm, w), lambda i: (i, 0))
    vrows = MLA_HEADS * MLA_VP
    is_one = (jnp.arange(vrows) % MLA_VP) == MLA_V
    vones = jnp.broadcast_to(is_one[:, None], (vrows, tm)).astype(F32)
    return pl.pallas_call(
        _in_proj_kernel,
        grid=(t // tm,),
        in_specs=[row(d), _resident(win.shape), _resident(gng.shape), _resident(gws.shape),
                  _resident(gbias.shape), _resident(qg.shape), _resident(wuq.shape),
                  _resident(kvg.shape), _resident(wuk.shape), _resident(wuvt.shape),
                  _resident(vones.shape), row(HEAD_PAD), row(HEAD_PAD), _resident(gmix_a.shape)],
        out_specs=[row(GM_WIDTH), row(MLA_HEADS * HEAD_PAD), row(MLA_HEADS * HEAD_PAD),
                   pl.BlockSpec((1, 1, vrows, tm), lambda i: (i // spb, i % spb, 0, 0)),
                   row(SSM_WIDTH)],
        out_shape=[jax.ShapeDtypeStruct((t, GM_WIDTH), BF16),
                   jax.ShapeDtypeStruct((t, MLA_HEADS * HEAD_PAD), BF16),
                   jax.ShapeDtypeStruct((t, MLA_HEADS * HEAD_PAD), BF16),
                   jax.ShapeDtypeStruct((batch, spb, vrows, tm), BF16),
                   jax.ShapeDtypeStruct((t, SSM_WIDTH), F32)],
        compiler_params=pltpu.CompilerParams(
            dimension_semantics=("arbitrary",), vmem_limit_bytes=VMEM_LIMIT),
        name="in_proj",
    )(x, win, gng, gws, gbias, qg, wuq, kvg, wuk, wuvt, vones, cos_t, sin_t, gmix_a)


def _attn_kernel(q_ref, k_ref, vt_ref, o_ref, s0_ref, s1_ref, bm0_ref, bm1_ref, m_ref, acc_ref, *, tq, nq):
    tk = tq

    def stage(score, acc_from):
        masked = False
        score_to = None
        if acc_from is not None:
            ja, (sa_ref, bma_ref) = acc_from
            m_old = [m_ref[hh] for hh in range(2)]
            m_new = [jnp.maximum(m_old[hh], bma_ref[hh]) for hh in range(2)]
        if score is not None:
            qb_s, js, (ss_ref, bms_ref), masked = score
            score_to = score
            q0 = pl.multiple_of(qb_s * tq, tq)
            qs = [q_ref[0, pl.ds(q0, tq), hh * HEAD_PAD:(hh + 1) * HEAD_PAD] for hh in range(2)]
            start = pl.multiple_of(js * tk, tk)
            bmax = [None, None]
        order = [(kind, c0) for c0 in range(0, tk, ATTN_CHUNK) for kind in ("s", "a")]
        for kind, c0 in order:
            for hh in range(2):
                rows = slice(hh * MLA_VP, (hh + 1) * MLA_VP)
                if kind == "s" and score_to is not None:
                    kb = k_ref[0, pl.ds(start + c0, ATTN_CHUNK), hh * HEAD_PAD:(hh + 1) * HEAD_PAD]
                    s = lax.dot_general(kb, qs[hh], (((1,), (1,)), ((), ())),
                                        preferred_element_type=F32)
                    if masked:
                        krow = lax.broadcasted_iota(jnp.int32, s.shape, 0) + c0
                        qcol = lax.broadcasted_iota(jnp.int32, s.shape, 1)
                        s = jnp.where(krow <= qcol, s, NEG_BIG)
                    ss_ref[hh, c0:c0 + ATTN_CHUNK, :] = s
                    cmax = jnp.max(s, axis=0, keepdims=True)
                    bmax[hh] = cmax if bmax[hh] is None else jnp.maximum(bmax[hh], cmax)
                if kind == "a" and acc_from is not None:
                    p = jnp.exp2(sa_ref[hh, c0:c0 + ATTN_CHUNK, :] - m_new[hh]).astype(BF16)
                    d = _dot(vt_ref[0, ja, rows, c0:c0 + ATTN_CHUNK], p)
                    if c0 == 0:
                        acc_ref[rows, :] = jnp.exp2(m_old[hh] - m_new[hh]) * acc_ref[rows, :] + d
                    else:
                        acc_ref[rows, :] += d
        for hh in range(2):
            if acc_from is not None:
                m_ref[hh] = m_new[hh]
            if score_to is not None:
                bms_ref[hh] = bmax[hh]

    bufs = ((s0_ref, bm0_ref), (s1_ref, bm1_ref))

    def reset():
        m_ref[...] = jnp.full(m_ref.shape, NEG_BIG, F32)
        acc_ref[...] = jnp.zeros(acc_ref.shape, F32)

    def finalize(qb):
        outs = []
        for hh in range(2):
            r0 = hh * MLA_VP
            inv_l = 1.0 / acc_ref[r0 + MLA_V:r0 + MLA_V + 1, :]
            outs.append(acc_ref[r0:r0 + MLA_V, :] * inv_l)
        o_ref[0, pl.ds(pl.multiple_of(qb * tq, tq), tq), :] = jnp.concatenate(outs, axis=0).T

    diag_buf = (0, 1, 1, 0)
    reset()
    stage((0, 0, bufs[0], True), None)

    def four_q_blocks(g, carry):
        for r in range(4):
            qb = 4 * g + r
            x = diag_buf[r]

            def pair(t, c, qb=qb, x=x):
                stage((qb, 2 * t, bufs[x ^ 1], False), (jnp.where(t == 0, qb, 2 * t - 1), bufs[x]))
                stage((qb, 2 * t + 1, bufs[x], False), (2 * t, bufs[x ^ 1]))
                return c

            lax.fori_loop(0, qb // 2, pair, 0)
            if r % 2 == 1:
                stage((qb, qb - 1, bufs[x ^ 1], False), (jnp.where(qb == 1, qb, qb - 2), bufs[x]))
                last_buf = x ^ 1
            else:
                last_buf = x
            assert diag_buf[(r + 1) % 4] == last_buf ^ 1
            stage((jnp.minimum(qb + 1, nq - 1), jnp.minimum(qb + 1, nq - 1), bufs[last_buf ^ 1], True),
                  (jnp.where(qb == 0, qb, qb - 1), bufs[last_buf]))
            finalize(qb)
            reset()
        return carry

    lax.fori_loop(0, nq // 4, four_q_blocks, 0)


def _attention(q, k, vt, batch, seq):
    tq, tk = ATTN_TQ, ATTN_TK
    nq = seq // tq
    assert nq % 4 == 0
    q3 = q.reshape(batch, seq, MLA_HEADS * HEAD_PAD)
    k3 = k.reshape(batch, seq, MLA_HEADS * HEAD_PAD)
    kern = functools.partial(_attn_kernel, tq=tq, nq=nq)
    o = pl.pallas_call(
        kern,
        grid=(batch, MLA_HEADS // 2),
        in_specs=[pl.BlockSpec((1, seq, 2 * HEAD_PAD), lambda b, h: (b, 0, h)),
                  pl.BlockSpec((1, seq, 2 * HEAD_PAD), lambda b, h: (b, 0, h)),
                  pl.BlockSpec((1, seq // tk, 2 * MLA_VP, tk), lambda b, h: (b, 0, h, 0))],
        out_specs=pl.BlockSpec((1, seq, 2 * MLA_V), lambda b, h: (b, 0, h)),
        out_shape=jax.ShapeDtypeStruct((batch, seq, MLA_WIDTH), F32),
        scratch_shapes=[pltpu.VMEM((2, tk, tq), F32), pltpu.VMEM((2, tk, tq), F32),
                        pltpu.VMEM((2, 1, tq), F32), pltpu.VMEM((2, 1, tq), F32),
                        pltpu.VMEM((2, 1, tq), F32), pltpu.VMEM((2 * MLA_VP, tq), F32)],
        compiler_params=pltpu.CompilerParams(
            dimension_semantics=("arbitrary", "arbitrary"), vmem_limit_bytes=VMEM_LIMIT),
        name="attention",
    )(q3, k3, vt)
    return o.reshape(batch * seq, MLA_WIDTH)


def _s5_prep_kernel(ar_ref, ai_ref, dt_ref, bre_ref, bim_ref, abr_ref, abi_ref, bb_ref):
    ar = ar_ref[...]
    ai = ai_ref[...]
    dt = jnp.exp(dt_ref[...])
    mag = jnp.exp(ar * dt)
    abr = mag * jnp.cos(ai * dt)
    abi = mag * jnp.sin(ai * dt)
    den = ar * ar + ai * ai
    cr = ((abr - 1.0) * ar + abi * ai) / den
    ci = (abi * ar - (abr - 1.0) * ai) / den
    abr_ref[...] = jnp.broadcast_to(abr, abr_ref.shape)
    abi_ref[...] = jnp.broadcast_to(abi, abi_ref.shape)
    bre = bre_ref[...]
    bim = bim_ref[...]
    r = lax.broadcasted_iota(jnp.int32, bre.shape, 0) // SSM_GROUP_CH
    c = lax.broadcasted_iota(jnp.int32, bre.shape, 1) // SSM_STATE
    same = r == c
    bb_ref[:, 0:SSM_NSTATE] = jnp.where(same, cr * bre - ci * bim, 0.0).astype(BF16)
    bb_ref[:, SSM_NSTATE:2 * SSM_NSTATE] = jnp.where(same, cr * bim + ci * bre, 0.0).astype(BF16)


def _s5_prep(a_re, a_im, log_dt, b_re, b_im, batch):
    n = SSM_NSTATE
    ar = a_re.reshape(1, n)
    ai = a_im.reshape(1, n)
    dt = jnp.repeat(log_dt, SSM_STATE).reshape(1, n)
    def expand(bm):
        cols = jnp.transpose(bm, (2, 0, 1)).reshape(SSM_GROUP_CH, n)
        return jnp.tile(cols, (SSM_GROUPS, 1))
    whole = lambda shp: pl.BlockSpec(shp, lambda: (0,) * len(shp))
    return pl.pallas_call(
        _s5_prep_kernel,
        in_specs=[whole((1, n)), whole((1, n)), whole((1, n)),
                  whole((SSM_WIDTH, n)), whole((SSM_WIDTH, n))],
        out_specs=[whole((batch, n)), whole((batch, n)), whole((SSM_WIDTH, 2 * n))],
        out_shape=[jax.ShapeDtypeStruct((batch, n), F32), jax.ShapeDtypeStruct((batch, n), F32),
                   jax.ShapeDtypeStruct((SSM_WIDTH, 2 * n), BF16)],
        name="s5_prep",
    )(ar, ai, dt, expand(b_re), expand(b_im))


def _s5_kernel(u_ref, abr_ref, abi_ref, bb_ref, cc_ref, d_ref, gw_ref, gb_ref, gmix_ref,
               y_ref, hre_ref, him_ref, cre_ref, cim_ref, *, batch, tl, ch):
    n = SSM_NSTATE
    rc = ch * batch
    nch = tl // ch

    @pl.when(pl.program_id(0) == 0)
    def _():
        cre_ref[...] = jnp.zeros(cre_ref.shape, F32)
        cim_ref[...] = jnp.zeros(cim_ref.shape, F32)

    ar = abr_ref[...]
    ai = abi_ref[...]

    def expand(k):
        r = slice(k * rc, (k + 1) * rc)
        ub = u_ref[r, :].astype(BF16)
        hre_ref[r, :] = _dot(ub, bb_ref[:, 0:n])
        him_ref[r, :] = _dot(ub, bb_ref[:, n:2 * n])

    def scan(k, hr, hi):
        for t in range(ch):
            r = slice(k * rc + t * batch, k * rc + (t + 1) * batch)
            nr = ar * hr - ai * hi + hre_ref[r, :]
            ni = ar * hi + ai * hr + him_ref[r, :]
            hre_ref[r, :] = nr
            him_ref[r, :] = ni
            hr, hi = nr, ni
        return hr, hi

    def contract(k):
        r = slice(k * rc, (k + 1) * rc)
        y = (_dot(hre_ref[r, :].astype(BF16), cc_ref[0:n, :])
             + _dot(him_ref[r, :].astype(BF16), cc_ref[n:2 * n, :])
             + d_ref[...] * u_ref[r, :])
        y = _gelu(y)
        y = y * _sigmoid(_dot(y.astype(BF16), gw_ref[...]) + gb_ref[...])
        y_ref[r, :] = (_rms_scale(y) * gmix_ref[...]).astype(BF16)

    hr, hi = cre_ref[...], cim_ref[...]
    expand(0)
    for k in range(nch):
        if k + 1 < nch:
            expand(k + 1)
        hr, hi = scan(k, hr, hi)
        if k >= 1:
            contract(k - 1)
    contract(nch - 1)
    cre_ref[...] = hr
    cim_ref[...] = hi


def _s5(u_tm, abr, abi, bb, cc, dskip, gw, gb, gmix_c, batch):
    rows, w = u_tm.shape
    tl = SSM_TL
    tr = tl * batch
    n = SSM_NSTATE
    kern = functools.partial(_s5_kernel, batch=batch, tl=tl, ch=SSM_CHUNK)
    return pl.pallas_call(
        kern,
        grid=(rows // tr,),
        in_specs=[pl.BlockSpec((tr, w), lambda i: (i, 0)),
                  _resident(abr.shape), _resident(abi.shape), _resident(bb.shape),
                  _resident(cc.shape), _resident(dskip.shape), _resident(gw.shape),
                  _resident(gb.shape), _resident(gmix_c.shape)],
        out_specs=pl.BlockSpec((tr, w), lambda i: (i, 0)),
        out_shape=jax.ShapeDtypeStruct((rows, w), BF16),
        scratch_shapes=[pltpu.VMEM((tr, n), F32), pltpu.VMEM((tr, n), F32),
                        pltpu.VMEM((batch, n), F32), pltpu.VMEM((batch, n), F32)],
        compiler_params=pltpu.CompilerParams(
            dimension_semantics=("arbitrary",), vmem_limit_bytes=VMEM_LIMIT),
        name="s5",
    )(u_tm, abr, abi, bb, cc, dskip, gw, gb, gmix_c)


def _out_ffn_kernel(x_ref, ya_ref, o_ref, yc_ref, gmix_b_ref, wa_ref, wb_ref, wc_ref,
                    g1_ref, b1_ref, wg_ref, wu_ref, wd_ref, g2_ref, b2_ref, out_ref,
                    *, alpha, f_chunk):
    x = x_ref[...]
    yb = (_rms_scale(o_ref[...]) * gmix_b_ref[...]).astype(BF16)
    y = _dot(ya_ref[...], wa_ref[...]) + _dot(yb, wb_ref[...]) + _dot(yc_ref[...], wc_ref[...])
    x1 = _layer_norm(alpha * x + y, g1_ref[...], b1_ref[...])
    out_ref[...] = _swiglu_ln(x1, wg_ref, wu_ref, wd_ref, g2_ref[...], b2_ref[...], alpha, f_chunk)


def _out_ffn(x, ya, o, yc, gmix_b, wa, wb, wc, g1, b1, wg, wu, wd, g2, b2, alpha):
    t, d = x.shape
    tm = ROW_TILE
    row = lambda w: pl.BlockSpec((tm, w), lambda i: (i, 0))
    kern = functools.partial(_out_ffn_kernel, alpha=alpha, f_chunk=_ffn_chunk(wg.shape[1]))
    return pl.pallas_call(
        kern,
        grid=(t // tm,),
        in_specs=[row(d), row(GM_WIDTH), row(MLA_WIDTH), row(SSM_WIDTH),
                  _resident(gmix_b.shape), _resident(wa.shape), _resident(wb.shape),
                  _resident(wc.shape), _resident(g1.shape), _resident(b1.shape),
                  _resident(wg.shape), _resident(wu.shape), _resident(wd.shape),
                  _resident(g2.shape), _resident(b2.shape)],
        out_specs=row(d),
        out_shape=jax.ShapeDtypeStruct((t, d), F32),
        compiler_params=pltpu.CompilerParams(
            dimension_semantics=("arbitrary",), vmem_limit_bytes=VMEM_LIMIT),
        name="out_ffn",
    )(x, ya, o, yc, gmix_b, wa, wb, wc, g1, b1, wg, wu, wd, g2, b2)


def _pack_w_in(w_in):
    d = w_in.shape[0]
    o1 = 2 * GM_WIDTH
    o2 = o1 + Q_LORA
    o3 = o2 + KV_LORA
    o4 = o3 + MLA_ROPE
    kr_tile = jnp.zeros((d, HEAD_PAD), w_in.dtype).at[:, MLA_NOPE:MLA_NOPE + MLA_ROPE].set(w_in[:, o3:o4])
    return jnp.concatenate([w_in[:, :o3], kr_tile, w_in[:, o4:]], axis=1).astype(BF16)


def _pack_w_uq(w_uq):
    r = w_uq.shape[0]
    w = w_uq.reshape(r, MLA_HEADS, MLA_NOPE + MLA_ROPE)
    w = jnp.pad(w, ((0, 0), (0, 0), (0, HEAD_PAD - MLA_NOPE - MLA_ROPE)))
    return w.reshape(r, MLA_HEADS * HEAD_PAD).astype(BF16)


def _pack_w_ukv(w_ukv):
    r = w_ukv.shape[0]
    w = w_ukv.reshape(r, MLA_HEADS, MLA_NOPE + MLA_V)
    wk = jnp.pad(w[:, :, :MLA_NOPE], ((0, 0), (0, 0), (0, HEAD_PAD - MLA_NOPE)))
    wvt = jnp.pad(jnp.transpose(w[:, :, MLA_NOPE:], (1, 2, 0)), ((0, 0), (0, MLA_VP - MLA_V), (0, 0)))
    return (wk.reshape(r, MLA_HEADS * HEAD_PAD).astype(BF16),
            wvt.reshape(MLA_HEADS * MLA_VP, r).astype(BF16))


def _pack_ssm_c(c_re, c_im):
    def blockdiag(cm):
        eye = jnp.eye(SSM_GROUPS, dtype=cm.dtype)
        m = jnp.transpose(cm, (0, 2, 1))[:, :, None, :] * eye[:, None, :, None]
        return m.reshape(SSM_NSTATE, SSM_WIDTH)
    return jnp.concatenate([blockdiag(c_re), -blockdiag(c_im)], axis=0).astype(BF16)


def kernel(x, positions, ln_g, ln_b, ffn1_w_gate, ffn1_w_up, ffn1_w_down, w_in, gmlp_norm_g, gmlp_ws, gmlp_bs, mla_q_norm_g, mla_w_uq, mla_kv_norm_g, mla_w_ukv, ssm_a_re, ssm_a_im, ssm_b_re, ssm_b_im, ssm_c_re, ssm_c_im, ssm_d, ssm_log_dt, ssm_glu_w, ssm_glu_b, mix_norm_g, w_out, ffn2_w_gate, ffn2_w_up, ffn2_w_down):
    batch, seq, d = x.shape
    depth = w_in.shape[0]
    t = batch * seq
    assert batch == 8, "the S5 kernel keeps the batch on the 8 sublanes of a vreg"
    assert seq % max(ROW_TILE, ATTN_TQ, SSM_TL) == 0 and ROW_TILE % GM_CHUNK == 0
    assert ATTN_TQ == ATTN_TK == ROW_TILE
    alpha = (2 * depth) ** 0.25

    cos_t, sin_t = _rope_tables(positions)
    xf = x.reshape(t, d)
    row = lambda a: a.reshape(1, -1)
    for l in range(depth):
        xf = _ffn_ln(xf, ffn1_w_gate[l].astype(BF16), ffn1_w_up[l].astype(BF16),
                     (0.5 * ffn1_w_down[l]).astype(BF16), row(ln_g[l, 0]), row(ln_b[l, 0]), alpha)

        wk, wvt = _pack_w_ukv(mla_w_ukv[l])
        gbias = jnp.repeat(gmlp_bs[l].T, GM_HEAD_DIM, axis=1)
        gmix = mix_norm_g[l]
        ya, q, k, vt, u = _in_proj(
            xf, _pack_w_in(w_in[l]), row(gmlp_norm_g[l]), gmlp_ws[l], gbias,
            row(mla_q_norm_g[l]), _pack_w_uq(mla_w_uq[l]), row(mla_kv_norm_g[l]), wk, wvt,
            cos_t, sin_t, row(gmix[:GM_WIDTH]), batch, seq)

        o = _attention(q, k, vt, batch, seq)

        abr, abi, bb = _s5_prep(ssm_a_re[l], ssm_a_im[l], ssm_log_dt[l], ssm_b_re[l], ssm_b_im[l], batch)
        u_tm = jnp.transpose(u.reshape(batch, seq, SSM_WIDTH), (1, 0, 2)).reshape(t, SSM_WIDTH)
        yc_tm = _s5(u_tm, abr, abi, bb, _pack_ssm_c(ssm_c_re[l], ssm_c_im[l]),
                    row(ssm_d[l]), ssm_glu_w[l].astype(BF16), row(ssm_glu_b[l]),
                    row(gmix[GM_WIDTH + MLA_WIDTH:]), batch)
        yc = jnp.transpose(yc_tm.reshape(seq, batch, SSM_WIDTH), (1, 0, 2)).reshape(t, SSM_WIDTH)

        wo = w_out[l].astype(BF16)
        xf = _out_ffn(xf, ya, o, yc, row(gmix[GM_WIDTH:GM_WIDTH + MLA_WIDTH]),
                      wo[:GM_WIDTH], wo[GM_WIDTH:GM_WIDTH + MLA_WIDTH], wo[GM_WIDTH + MLA_WIDTH:],
                      row(ln_g[l, 1]), row(ln_b[l, 1]),
                      ffn2_w_gate[l].astype(BF16), ffn2_w_up[l].astype(BF16),
                      (0.5 * ffn2_w_down[l]).astype(BF16), row(ln_g[l, 2]), row(ln_b[l, 2]), alpha)
    return xf.reshape(batch, seq, d)
```

```python
import functools
import math

import jax
import jax.numpy as jnp
from jax import lax
from jax.experimental import pallas as pl
from jax.experimental.pallas import tpu as pltpu

F32 = jnp.float32
BF16 = jnp.bfloat16

GM_HEADS = 4
GM_HEAD_DIM = 64
GM_WIDTH = GM_HEADS * GM_HEAD_DIM
GM_CHUNK = 128
MLA_HEADS = 8
MLA_NOPE = 64
MLA_ROPE = 32
MLA_V = 64
MLA_WIDTH = MLA_HEADS * MLA_V
Q_LORA = 256
KV_LORA = 128
ROPE_BASE = 10000.0
SSM_GROUPS = 16
SSM_GROUP_CH = 16
SSM_WIDTH = SSM_GROUPS * SSM_GROUP_CH
SSM_STATE = 64
SSM_NSTATE = SSM_GROUPS * SSM_STATE
LN_EPS = 1e-5
RMS_EPS = 1e-6
NEG_BIG = -1e30

LANES = 128
BF16_SUBLANES = 16
HEAD_PAD = LANES
MLA_VP = MLA_V + BF16_SUBLANES
SOFTMAX_C = (MLA_NOPE + MLA_ROPE) ** -0.5 * math.log2(math.e)
VMEM_LIMIT = 56 * 1024 * 1024

ROW_TILE = 512
ATTN_TQ = 512
ATTN_TK = 512
ATTN_CHUNK = ATTN_TK
SSM_TL = 128
SSM_CHUNK = 32


def _gelu(x):
    c = math.sqrt(2.0 / math.pi)
    return 0.5 * x * (1.0 + jnp.tanh(c * (x + 0.044715 * (x * x * x))))


def _sigmoid(x):
    return 1.0 / (1.0 + jnp.exp(-x))


def _layer_norm(r, g, b):
    mu = jnp.mean(r, axis=-1, keepdims=True)
    d = r - mu
    var = jnp.mean(d * d, axis=-1, keepdims=True)
    return d * lax.rsqrt(var + LN_EPS) * g + b


def _rms_scale(y):
    return y * lax.rsqrt(jnp.mean(y * y, axis=-1, keepdims=True) + RMS_EPS)


def _dot(a, b):
    return jnp.dot(a, b, preferred_element_type=F32)


def _swiglu_ln(x, wg_ref, wu_ref, wd_ref, g, b, alpha, f_chunk):
    xb = x.astype(BF16)
    d_ff = wg_ref.shape[1]
    acts = []
    for c0 in range(0, d_ff, f_chunk):
        hg = _dot(xb, wg_ref[:, c0:c0 + f_chunk])
        hu = _dot(xb, wu_ref[:, c0:c0 + f_chunk])
        acts.append(((hg * _sigmoid(hg)) * hu).astype(BF16))
    half = x.shape[0] // 2
    outs = []
    for r0 in (0, half):
        y = None
        for ci, c0 in enumerate(range(0, d_ff, f_chunk)):
            part = _dot(acts[ci][r0:r0 + half, :], wd_ref[c0:c0 + f_chunk, :])
            y = part if y is None else y + part
        outs.append(_layer_norm(alpha * x[r0:r0 + half, :] + y, g, b))
    return jnp.concatenate(outs, axis=0)


def _ffn_ln_kernel(x_ref, wg_ref, wu_ref, wd_ref, g_ref, b_ref, o_ref, *, alpha, f_chunk):
    o_ref[...] = _swiglu_ln(x_ref[...], wg_ref, wu_ref, wd_ref, g_ref[...], b_ref[...], alpha, f_chunk)


def _resident(shape):
    nd = len(shape)
    return pl.BlockSpec(shape, lambda *_: (0,) * nd, pipeline_mode=pl.Buffered(1))


def _ffn_chunk(d_ff):
    return d_ff // 2 if (d_ff // 2) % LANES == 0 else d_ff


def _ffn_ln(x, wg, wu, wd, g, b, alpha):
    t, d = x.shape
    tm = ROW_TILE
    kern = functools.partial(_ffn_ln_kernel, alpha=alpha, f_chunk=_ffn_chunk(wg.shape[1]))
    return pl.pallas_call(
        kern,
        grid=(t // tm,),
        in_specs=[
            pl.BlockSpec((tm, d), lambda i: (i, 0)),
            _resident(wg.shape), _resident(wu.shape), _resident(wd.shape),
            _resident(g.shape), _resident(b.shape),
        ],
        out_specs=pl.BlockSpec((tm, d), lambda i: (i, 0)),
        out_shape=jax.ShapeDtypeStruct((t, d), F32),
        compiler_params=pltpu.CompilerParams(
            dimension_semantics=("arbitrary",), vmem_limit_bytes=VMEM_LIMIT),
        name="ffn_ln",
    )(x, wg, wu, wd, g, b)


def _rope_table_kernel(pos_ref, freq_ref, cos_ref, sin_ref):
    ang = pos_ref[...].astype(F32) * freq_ref[...]
    lane = lax.broadcasted_iota(jnp.int32, ang.shape, 1)
    half = MLA_ROPE // 2
    c = jnp.cos(ang)
    s = jnp.sin(ang)
    in_rope = (lane >= MLA_NOPE) & (lane < MLA_NOPE + MLA_ROPE)
    cos_ref[...] = jnp.where(lane < MLA_NOPE, 1.0, jnp.where(in_rope, c, 0.0))
    sin_ref[...] = jnp.where(in_rope, jnp.where(lane < MLA_NOPE + half, -s, s), 0.0)


def _rope_tables(positions):
    b, s = positions.shape
    t = b * s
    tm = ROW_TILE
    half = MLA_ROPE // 2
    inv_freq = 1.0 / (ROPE_BASE ** (jnp.arange(0, MLA_ROPE, 2, dtype=F32) / MLA_ROPE))
    freq_row = jnp.zeros((1, HEAD_PAD), F32)
    freq_row = freq_row.at[0, MLA_NOPE:MLA_NOPE + half].set(inv_freq)
    freq_row = freq_row.at[0, MLA_NOPE + half:MLA_NOPE + MLA_ROPE].set(inv_freq)
    pos = positions.reshape(t, 1)
    return pl.pallas_call(
        _rope_table_kernel,
        grid=(t // tm,),
        in_specs=[pl.BlockSpec((tm, 1), lambda i: (i, 0)),
                  pl.BlockSpec((1, HEAD_PAD), lambda i: (0, 0))],
        out_specs=[pl.BlockSpec((tm, HEAD_PAD), lambda i: (i, 0))] * 2,
        out_shape=[jax.ShapeDtypeStruct((t, HEAD_PAD), F32)] * 2,
        compiler_params=pltpu.CompilerParams(dimension_semantics=("arbitrary",)),
        name="rope_tables",
    )(pos, freq_row)


def _rope_tile(x, partner, cos_t, sin_t):
    return x * cos_t + partner * sin_t


def _in_proj_kernel(x_ref, win_ref, gng_ref, gws_ref, gbias_ref, qg_ref, wuq_ref, wuqp_ref, kvg_ref,
                    wuk_ref, wuvt_ref, vones_ref, cos_ref, sin_ref, gmix_a_ref,
                    ya_ref, q_ref, k_ref, vt_ref, u_ref):
    tm = x_ref.shape[0]
    xb = x_ref[...].astype(BF16)
    h = _dot(xb, win_ref[...])
    o_v = GM_WIDTH
    o_q = 2 * GM_WIDTH
    o_kv = o_q + Q_LORA
    o_kr = o_kv + KV_LORA
    o_krp = o_kr + HEAD_PAD
    o_ssm = o_krp + HEAD_PAD

    u_ref[...] = h[:, o_ssm:o_ssm + SSM_WIDTH]

    ug = _gelu(h[:, 0:GM_WIDTH])
    vg = _gelu(h[:, o_v:o_v + GM_WIDTH])
    lane_g = lax.broadcasted_iota(jnp.int32, (1, GM_WIDTH), 1)
    head_masks = [(lane_g >= hd * GM_HEAD_DIM) & (lane_g < (hd + 1) * GM_HEAD_DIM)
                  for hd in range(GM_HEADS)]

    def seg_mean(a):
        out = jnp.zeros_like(a)
        for m in head_masks:
            s = jnp.sum(jnp.where(m, a, 0.0), axis=-1, keepdims=True) * (1.0 / GM_HEAD_DIM)
            out = jnp.where(m, s, out)
        return out

    dv = vg - seg_mean(vg)
    vn = dv * lax.rsqrt(seg_mean(dv * dv) + LN_EPS) * gng_ref[...]
    vnb = vn.astype(BF16)
    row = lax.broadcasted_iota(jnp.int32, (GM_CHUNK, GM_CHUNK), 0)
    col = lax.broadcasted_iota(jnp.int32, (GM_CHUNK, GM_CHUNK), 1)
    tril = col <= row
    wc = [jnp.where(tril, gws_ref[hd], 0.0).astype(BF16) for hd in range(GM_HEADS)]
    gbias = gbias_ref[...]
    gmix_a = gmix_a_ref[...]
    for c0 in range(0, tm, GM_CHUNK):
        vc = vnb[c0:c0 + GM_CHUNK, :]
        z = jnp.zeros((GM_CHUNK, GM_WIDTH), F32)
        for hd in range(GM_HEADS):
            z = jnp.where(head_masks[hd], _dot(wc[hd], vc), z)
        ya = ug[c0:c0 + GM_CHUNK, :] * (z + gbias)
        ya_ref[c0:c0 + GM_CHUNK, :] = (_rms_scale(ya) * gmix_a).astype(BF16)

    cos_t = cos_ref[...]
    sin_t = sin_ref[...]
    cq = h[:, o_q:o_q + Q_LORA]
    cqn = (_rms_scale(cq) * qg_ref[...]).astype(BF16)
    q = _dot(cqn, wuq_ref[...])
    qp = _dot(cqn, wuqp_ref[...])
    for hd in range(MLA_HEADS):
        cols = slice(hd * HEAD_PAD, (hd + 1) * HEAD_PAD)
        q_ref[:, cols] = (_rope_tile(q[:, cols], qp[:, cols], cos_t, sin_t) * SOFTMAX_C).astype(BF16)
    ckv = h[:, o_kv:o_kv + KV_LORA]
    ckvn = (_rms_scale(ckv) * kvg_ref[...]).astype(BF16)
    kn = _dot(ckvn, wuk_ref[...])
    kr = _rope_tile(h[:, o_kr:o_kr + HEAD_PAD], h[:, o_krp:o_krp + HEAD_PAD], cos_t, sin_t)
    for hd in range(MLA_HEADS):
        k_ref[:, hd * HEAD_PAD:(hd + 1) * HEAD_PAD] = (kn[:, hd * HEAD_PAD:(hd + 1) * HEAD_PAD] + kr).astype(BF16)
    vt = lax.dot_general(wuvt_ref[...], ckvn, (((1,), (1,)), ((), ())),
                         preferred_element_type=F32)
    vt_ref[0, 0] = (vt + vones_ref[...]).astype(BF16)


def _in_proj(x, win, gng, gws, gbias, qg, wuq, wuqp, kvg, wuk, wuvt, cos_t, sin_t, gmix_a, batch, seq):
    t, d = x.shape
    tm = ROW_TILE
    spb = seq // tm
    row = lambda w: pl.BlockSpec((tm, w), lambda i: (i, 0))
    vrows = MLA_HEADS * MLA_VP
    is_one = (jnp.arange(vrows) % MLA_VP) == MLA_V
    vones = jnp.broadcast_to(is_one[:, None], (vrows, tm)).astype(F32)
    return pl.pallas_call(
        _in_proj_kernel,
        grid=(t // tm,),
        in_specs=[row(d), _resident(win.shape), _resident(gng.shape), _resident(gws.shape),
                  _resident(gbias.shape), _resident(qg.shape), _resident(wuq.shape),
                  _resident(wuqp.shape), _resident(kvg.shape), _resident(wuk.shape), _resident(wuvt.shape),
                  _resident(vones.shape), row(HEAD_PAD), row(HEAD_PAD), _resident(gmix_a.shape)],
        out_specs=[row(GM_WIDTH), row(MLA_HEADS * HEAD_PAD), row(MLA_HEADS * HEAD_PAD),
                   pl.BlockSpec((1, 1, vrows, tm), lambda i: (i // spb, i % spb, 0, 0)),
                   row(SSM_WIDTH)],
        out_shape=[jax.ShapeDtypeStruct((t, GM_WIDTH), BF16),
                   jax.ShapeDtypeStruct((t, MLA_HEADS * HEAD_PAD), BF16),
                   jax.ShapeDtypeStruct((t, MLA_HEADS * HEAD_PAD), BF16),
                   jax.ShapeDtypeStruct((batch, spb, vrows, tm), BF16),
                   jax.ShapeDtypeStruct((t, SSM_WIDTH), F32)],
        compiler_params=pltpu.CompilerParams(
            dimension_semantics=("arbitrary",), vmem_limit_bytes=VMEM_LIMIT),
        name="in_proj",
    )(x, win, gng, gws, gbias, qg, wuq, wuqp, kvg, wuk, wuvt, vones, cos_t, sin_t, gmix_a)


def _attn_kernel(q_ref, k_ref, vt_ref, o_ref, s0_ref, s1_ref, bm0_ref, bm1_ref, m_ref, acc_ref, *, tq, nq):
    tk = tq

    def stage(score, acc_from):
        masked = False
        score_to = None
        if acc_from is not None:
            ja, (sa_ref, bma_ref) = acc_from
            m_old = [m_ref[hh] for hh in range(2)]
            m_new = [jnp.maximum(m_old[hh], bma_ref[hh]) for hh in range(2)]
        if score is not None:
            qb_s, js, (ss_ref, bms_ref), masked = score
            score_to = score
            q0 = pl.multiple_of(qb_s * tq, tq)
            qs = [q_ref[0, pl.ds(q0, tq), hh * HEAD_PAD:(hh + 1) * HEAD_PAD] for hh in range(2)]
            start = pl.multiple_of(js * tk, tk)
            bmax = [None, None]
        order = [(kind, c0) for c0 in range(0, tk, ATTN_CHUNK) for kind in ("s", "a")]
        for kind, c0 in order:
            for hh in range(2):
                rows = slice(hh * MLA_VP, (hh + 1) * MLA_VP)
                if kind == "s" and score_to is not None:
                    kb = k_ref[0, pl.ds(start + c0, ATTN_CHUNK), hh * HEAD_PAD:(hh + 1) * HEAD_PAD]
                    s = lax.dot_general(kb, qs[hh], (((1,), (1,)), ((), ())),
                                        preferred_element_type=F32)
                    if masked:
                        krow = lax.broadcasted_iota(jnp.int32, s.shape, 0) + c0
                        qcol = lax.broadcasted_iota(jnp.int32, s.shape, 1)
                        s = jnp.where(krow <= qcol, s, NEG_BIG)
                    ss_ref[hh, c0:c0 + ATTN_CHUNK, :] = s
                    cmax = jnp.max(s, axis=0, keepdims=True)
                    bmax[hh] = cmax if bmax[hh] is None else jnp.maximum(bmax[hh], cmax)
                if kind == "a" and acc_from is not None:
                    p = jnp.exp2(sa_ref[hh, c0:c0 + ATTN_CHUNK, :] - m_new[hh]).astype(BF16)
                    d = _dot(vt_ref[0, ja, rows, c0:c0 + ATTN_CHUNK], p)
                    if c0 == 0:
                        acc_ref[rows, :] = jnp.exp2(m_old[hh] - m_new[hh]) * acc_ref[rows, :] + d
                    else:
                        acc_ref[rows, :] += d
        for hh in range(2):
            if acc_from is not None:
                m_ref[hh] = m_new[hh]
            if score_to is not None:
                bms_ref[hh] = bmax[hh]

    bufs = ((s0_ref, bm0_ref), (s1_ref, bm1_ref))

    def reset():
        m_ref[...] = jnp.full(m_ref.shape, NEG_BIG, F32)
        acc_ref[...] = jnp.zeros(acc_ref.shape, F32)

    def finalize(qb):
        outs = []
        for hh in range(2):
            r0 = hh * MLA_VP
            inv_l = 1.0 / acc_ref[r0 + MLA_V:r0 + MLA_V + 1, :]
            outs.append(acc_ref[r0:r0 + MLA_V, :] * inv_l)
        o_ref[0, pl.ds(pl.multiple_of(qb * tq, tq), tq), :] = jnp.concatenate(outs, axis=0).T

    diag_buf = (0, 1, 1, 0)
    reset()
    stage((0, 0, bufs[0], True), None)

    def four_q_blocks(g, carry):
        for r in range(4):
            qb = 4 * g + r
            x = diag_buf[r]

            def pair(t, c, qb=qb, x=x):
                stage((qb, 2 * t, bufs[x ^ 1], False), (jnp.where(t == 0, qb, 2 * t - 1), bufs[x]))
                stage((qb, 2 * t + 1, bufs[x], False), (2 * t, bufs[x ^ 1]))
                return c

            lax.fori_loop(0, qb // 2, pair, 0)
            if r % 2 == 1:
                stage((qb, qb - 1, bufs[x ^ 1], False), (jnp.where(qb == 1, qb, qb - 2), bufs[x]))
                last_buf = x ^ 1
            else:
                last_buf = x
            assert diag_buf[(r + 1) % 4] == last_buf ^ 1
            stage((jnp.minimum(qb + 1, nq - 1), jnp.minimum(qb + 1, nq - 1), bufs[last_buf ^ 1], True),
                  (jnp.where(qb == 0, qb, qb - 1), bufs[last_buf]))
            finalize(qb)
            reset()
        return carry

    lax.fori_loop(0, nq // 4, four_q_blocks, 0)


def _attention(q, k, vt, batch, seq):
    tq, tk = ATTN_TQ, ATTN_TK
    nq = seq // tq
    assert nq % 4 == 0
    q3 = q.reshape(batch, seq, MLA_HEADS * HEAD_PAD)
    k3 = k.reshape(batch, seq, MLA_HEADS * HEAD_PAD)
    kern = functools.partial(_attn_kernel, tq=tq, nq=nq)
    o = pl.pallas_call(
        kern,
        grid=(batch, MLA_HEADS // 2),
        in_specs=[pl.BlockSpec((1, seq, 2 * HEAD_PAD), lambda b, h: (b, 0, h)),
                  pl.BlockSpec((1, seq, 2 * HEAD_PAD), lambda b, h: (b, 0, h)),
                  pl.BlockSpec((1, seq // tk, 2 * MLA_VP, tk), lambda b, h: (b, 0, h, 0))],
        out_specs=pl.BlockSpec((1, seq, 2 * MLA_V), lambda b, h: (b, 0, h)),
        out_shape=jax.ShapeDtypeStruct((batch, seq, MLA_WIDTH), F32),
        scratch_shapes=[pltpu.VMEM((2, tk, tq), F32), pltpu.VMEM((2, tk, tq), F32),
                        pltpu.VMEM((2, 1, tq), F32), pltpu.VMEM((2, 1, tq), F32),
                        pltpu.VMEM((2, 1, tq), F32), pltpu.VMEM((2 * MLA_VP, tq), F32)],
        compiler_params=pltpu.CompilerParams(
            dimension_semantics=("arbitrary", "arbitrary"), vmem_limit_bytes=VMEM_LIMIT),
        name="attention",
    )(q3, k3, vt)
    return o.reshape(batch * seq, MLA_WIDTH)


def _s5_prep_kernel(ar_ref, ai_ref, dt_ref, bre_ref, bim_ref, abr_ref, abi_ref, bb_ref):
    ar = ar_ref[...]
    ai = ai_ref[...]
    dt = jnp.exp(dt_ref[...])
    mag = jnp.exp(ar * dt)
    abr = mag * jnp.cos(ai * dt)
    abi = mag * jnp.sin(ai * dt)
    den = ar * ar + ai * ai
    cr = ((abr - 1.0) * ar + abi * ai) / den
    ci = (abi * ar - (abr - 1.0) * ai) / den
    abr_ref[...] = jnp.broadcast_to(abr, abr_ref.shape)
    abi_ref[...] = jnp.broadcast_to(abi, abi_ref.shape)
    bre = bre_ref[...]
    bim = bim_ref[...]
    r = lax.broadcasted_iota(jnp.int32, bre.shape, 0) // SSM_GROUP_CH
    c = lax.broadcasted_iota(jnp.int32, bre.shape, 1) // SSM_STATE
    same = r == c
    bb_ref[:, 0:SSM_NSTATE] = jnp.where(same, cr * bre - ci * bim, 0.0).astype(BF16)
    bb_ref[:, SSM_NSTATE:2 * SSM_NSTATE] = jnp.where(same, cr * bim + ci * bre, 0.0).astype(BF16)


def _s5_prep(a_re, a_im, log_dt, b_re, b_im, batch):
    n = SSM_NSTATE
    ar = a_re.reshape(1, n)
    ai = a_im.reshape(1, n)
    dt = jnp.repeat(log_dt, SSM_STATE).reshape(1, n)
    def expand(bm):
        cols = jnp.transpose(bm, (2, 0, 1)).reshape(SSM_GROUP_CH, n)
        return jnp.tile(cols, (SSM_GROUPS, 1))
    whole = lambda shp: pl.BlockSpec(shp, lambda: (0,) * len(shp))
    return pl.pallas_call(
        _s5_prep_kernel,
        in_specs=[whole((1, n)), whole((1, n)), whole((1, n)),
                  whole((SSM_WIDTH, n)), whole((SSM_WIDTH, n))],
        out_specs=[whole((batch, n)), whole((batch, n)), whole((SSM_WIDTH, 2 * n))],
        out_shape=[jax.ShapeDtypeStruct((batch, n), F32), jax.ShapeDtypeStruct((batch, n), F32),
                   jax.ShapeDtypeStruct((SSM_WIDTH, 2 * n), BF16)],
        name="s5_prep",
    )(ar, ai, dt, expand(b_re), expand(b_im))


def _s5_kernel(u_ref, abr_ref, abi_ref, bb_ref, cc_ref, d_ref, gw_ref, gb_ref, gmix_ref,
               y_ref, hre_ref, him_ref, cre_ref, cim_ref, *, batch, tl, ch):
    n = SSM_NSTATE
    rc = ch * batch
    nch = tl // ch

    @pl.when(pl.program_id(0) == 0)
    def _():
        cre_ref[...] = jnp.zeros(cre_ref.shape, F32)
        cim_ref[...] = jnp.zeros(cim_ref.shape, F32)

    ar = abr_ref[...]
    ai = abi_ref[...]

    def expand(k):
        r = slice(k * rc, (k + 1) * rc)
        ub = u_ref[r, :].astype(BF16)
        hre_ref[r, :] = _dot(ub, bb_ref[:, 0:n])
        him_ref[r, :] = _dot(ub, bb_ref[:, n:2 * n])

    def scan(k, hr, hi):
        for t in range(ch):
            r = slice(k * rc + t * batch, k * rc + (t + 1) * batch)
            nr = ar * hr - ai * hi + hre_ref[r, :]
            ni = ar * hi + ai * hr + him_ref[r, :]
            hre_ref[r, :] = nr
            him_ref[r, :] = ni
            hr, hi = nr, ni
        return hr, hi

    def contract(k):
        r = slice(k * rc, (k + 1) * rc)
        y = (_dot(hre_ref[r, :].astype(BF16), cc_ref[0:n, :])
             + _dot(him_ref[r, :].astype(BF16), cc_ref[n:2 * n, :])
             + d_ref[...] * u_ref[r, :])
        y = _gelu(y)
        y = y * _sigmoid(_dot(y.astype(BF16), gw_ref[...]) + gb_ref[...])
        y_ref[r, :] = (_rms_scale(y) * gmix_ref[...]).astype(BF16)

    hr, hi = cre_ref[...], cim_ref[...]
    expand(0)
    for k in range(nch):
        if k + 1 < nch:
            expand(k + 1)
        hr, hi = scan(k, hr, hi)
        if k >= 1:
            contract(k - 1)
    contract(nch - 1)
    cre_ref[...] = hr
    cim_ref[...] = hi


def _s5(u_tm, abr, abi, bb, cc, dskip, gw, gb, gmix_c, batch):
    rows, w = u_tm.shape
    tl = SSM_TL
    tr = tl * batch
    n = SSM_NSTATE
    kern = functools.partial(_s5_kernel, batch=batch, tl=tl, ch=SSM_CHUNK)
    return pl.pallas_call(
        kern,
        grid=(rows // tr,),
        in_specs=[pl.BlockSpec((tr, w), lambda i: (i, 0)),
                  _resident(abr.shape), _resident(abi.shape), _resident(bb.shape),
                  _resident(cc.shape), _resident(dskip.shape), _resident(gw.shape),
                  _resident(gb.shape), _resident(gmix_c.shape)],
        out_specs=pl.BlockSpec((tr, w), lambda i: (i, 0)),
        out_shape=jax.ShapeDtypeStruct((rows, w), BF16),
        scratch_shapes=[pltpu.VMEM((tr, n), F32), pltpu.VMEM((tr, n), F32),
                        pltpu.VMEM((batch, n), F32), pltpu.VMEM((batch, n), F32)],
        compiler_params=pltpu.CompilerParams(
            dimension_semantics=("arbitrary",), vmem_limit_bytes=VMEM_LIMIT),
        name="s5",
    )(u_tm, abr, abi, bb, cc, dskip, gw, gb, gmix_c)


def _out_ffn_kernel(x_ref, ya_ref, o_ref, yc_ref, gmix_b_ref, wa_ref, wb_ref, wc_ref,
                    g1_ref, b1_ref, wg_ref, wu_ref, wd_ref, g2_ref, b2_ref, out_ref,
                    *, alpha, f_chunk):
    x = x_ref[...]
    yb = (_rms_scale(o_ref[...]) * gmix_b_ref[...]).astype(BF16)
    y = _dot(ya_ref[...], wa_ref[...]) + _dot(yb, wb_ref[...]) + _dot(yc_ref[...], wc_ref[...])
    x1 = _layer_norm(alpha * x + y, g1_ref[...], b1_ref[...])
    out_ref[...] = _swiglu_ln(x1, wg_ref, wu_ref, wd_ref, g2_ref[...], b2_ref[...], alpha, f_chunk)


def _out_ffn(x, ya, o, yc, gmix_b, wa, wb, wc, g1, b1, wg, wu, wd, g2, b2, alpha):
    t, d = x.shape
    tm = ROW_TILE
    row = lambda w: pl.BlockSpec((tm, w), lambda i: (i, 0))
    kern = functools.partial(_out_ffn_kernel, alpha=alpha, f_chunk=_ffn_chunk(wg.shape[1]))
    return pl.pallas_call(
        kern,
        grid=(t // tm,),
        in_specs=[row(d), row(GM_WIDTH), row(MLA_WIDTH), row(SSM_WIDTH),
                  _resident(gmix_b.shape), _resident(wa.shape), _resident(wb.shape),
                  _resident(wc.shape), _resident(g1.shape), _resident(b1.shape),
                  _resident(wg.shape), _resident(wu.shape), _resident(wd.shape),
                  _resident(g2.shape), _resident(b2.shape)],
        out_specs=row(d),
        out_shape=jax.ShapeDtypeStruct((t, d), F32),
        compiler_params=pltpu.CompilerParams(
            dimension_semantics=("arbitrary",), vmem_limit_bytes=VMEM_LIMIT),
        name="out_ffn",
    )(x, ya, o, yc, gmix_b, wa, wb, wc, g1, b1, wg, wu, wd, g2, b2)


def _pack_w_in(w_in):
    d = w_in.shape[0]
    half = MLA_ROPE // 2
    o1 = 2 * GM_WIDTH
    o2 = o1 + Q_LORA
    o3 = o2 + KV_LORA
    o4 = o3 + MLA_ROPE
    zeros = jnp.zeros((d, HEAD_PAD), w_in.dtype)
    kr_tile = zeros.at[:, MLA_NOPE:MLA_NOPE + MLA_ROPE].set(w_in[:, o3:o4])
    swapped = jnp.concatenate([w_in[:, o3 + half:o4], w_in[:, o3:o3 + half]], axis=1)
    krp_tile = zeros.at[:, MLA_NOPE:MLA_NOPE + MLA_ROPE].set(swapped)
    return jnp.concatenate([w_in[:, :o3], kr_tile, krp_tile, w_in[:, o4:]], axis=1).astype(BF16)


def _pack_w_uq(w_uq):
    r = w_uq.shape[0]
    half = MLA_ROPE // 2
    w = w_uq.reshape(r, MLA_HEADS, MLA_NOPE + MLA_ROPE)
    pad = HEAD_PAD - MLA_NOPE - MLA_ROPE
    wq = jnp.pad(w, ((0, 0), (0, 0), (0, pad)))
    swapped = jnp.concatenate([w[:, :, MLA_NOPE + half:], w[:, :, MLA_NOPE:MLA_NOPE + half]], axis=2)
    wp = jnp.pad(swapped, ((0, 0), (0, 0), (MLA_NOPE, pad)))
    shape = (r, MLA_HEADS * HEAD_PAD)
    return wq.reshape(shape).astype(BF16), wp.reshape(shape).astype(BF16)


def _pack_w_ukv(w_ukv):
    r = w_ukv.shape[0]
    w = w_ukv.reshape(r, MLA_HEADS, MLA_NOPE + MLA_V)
    wk = jnp.pad(w[:, :, :MLA_NOPE], ((0, 0), (0, 0), (0, HEAD_PAD - MLA_NOPE)))
    wvt = jnp.pad(jnp.transpose(w[:, :, MLA_NOPE:], (1, 2, 0)), ((0, 0), (0, MLA_VP - MLA_V), (0, 0)))
    return (wk.reshape(r, MLA_HEADS * HEAD_PAD).astype(BF16),
            wvt.reshape(MLA_HEADS * MLA_VP, r).astype(BF16))


def _pack_ssm_c(c_re, c_im):
    def blockdiag(cm):
        eye = jnp.eye(SSM_GROUPS, dtype=cm.dtype)
        m = jnp.transpose(cm, (0, 2, 1))[:, :, None, :] * eye[:, None, :, None]
        return m.reshape(SSM_NSTATE, SSM_WIDTH)
    return jnp.concatenate([blockdiag(c_re), -blockdiag(c_im)], axis=0).astype(BF16)


def kernel(x, positions, ln_g, ln_b, ffn1_w_gate, ffn1_w_up, ffn1_w_down, w_in, gmlp_norm_g, gmlp_ws, gmlp_bs, mla_q_norm_g, mla_w_uq, mla_kv_norm_g, mla_w_ukv, ssm_a_re, ssm_a_im, ssm_b_re, ssm_b_im, ssm_c_re, ssm_c_im, ssm_d, ssm_log_dt, ssm_glu_w, ssm_glu_b, mix_norm_g, w_out, ffn2_w_gate, ffn2_w_up, ffn2_w_down):
    batch, seq, d = x.shape
    depth = w_in.shape[0]
    t = batch * seq
    assert batch == 8, "the S5 kernel keeps the batch on the 8 sublanes of a vreg"
    assert seq % max(ROW_TILE, ATTN_TQ, SSM_TL) == 0 and ROW_TILE % GM_CHUNK == 0
    assert ATTN_TQ == ATTN_TK == ROW_TILE
    alpha = (2 * depth) ** 0.25

    cos_t, sin_t = _rope_tables(positions)
    xf = x.reshape(t, d)
    row = lambda a: a.reshape(1, -1)
    for l in range(depth):
        xf = _ffn_ln(xf, ffn1_w_gate[l].astype(BF16), ffn1_w_up[l].astype(BF16),
                     (0.5 * ffn1_w_down[l]).astype(BF16), row(ln_g[l, 0]), row(ln_b[l, 0]), alpha)

        wk, wvt = _pack_w_ukv(mla_w_ukv[l])
        gbias = jnp.repeat(gmlp_bs[l].T, GM_HEAD_DIM, axis=1)
        gmix = mix_norm_g[l]
        ya, q, k, vt, u = _in_proj(
            xf, _pack_w_in(w_in[l]), row(gmlp_norm_g[l]), gmlp_ws[l], gbias,
            row(mla_q_norm_g[l]), *_pack_w_uq(mla_w_uq[l]), row(mla_kv_norm_g[l]), wk, wvt,
            cos_t, sin_t, row(gmix[:GM_WIDTH]), batch, seq)

        o = _attention(q, k, vt, batch, seq)

        abr, abi, bb = _s5_prep(ssm_a_re[l], ssm_a_im[l], ssm_log_dt[l], ssm_b_re[l], ssm_b_im[l], batch)
        u_tm = jnp.transpose(u.reshape(batch, seq, SSM_WIDTH), (1, 0, 2)).reshape(t, SSM_WIDTH)
        yc_tm = _s5(u_tm, abr, abi, bb, _pack_ssm_c(ssm_c_re[l], ssm_c_im[l]),
                    row(ssm_d[l]), ssm_glu_w[l].astype(BF16), row(ssm_glu_b[l]),
                    row(gmix[GM_WIDTH + MLA_WIDTH:]), batch)
        yc = jnp.transpose(yc_tm.reshape(seq, batch, SSM_WIDTH), (1, 0, 2)).reshape(t, SSM_WIDTH)

        wo = w_out[l].astype(BF16)
        xf = _out_ffn(xf, ya, o, yc, row(gmix[GM_WIDTH:GM_WIDTH + MLA_WIDTH]),
                      wo[:GM_WIDTH], wo[GM_WIDTH:GM_WIDTH + MLA_WIDTH], wo[GM_WIDTH + MLA_WIDTH:],
                      row(ln_g[l, 1]), row(ln_b[l, 1]),
                      ffn2_w_gate[l].astype(BF16), ffn2_w_up[l].astype(BF16),
                      (0.5 * ffn2_w_down[l]).astype(BF16), row(ln_g[l, 2]), row(ln_b[l, 2]), alpha)
    return xf.reshape(batch, seq, d)
```

```python
import functools
import math

import jax
import jax.numpy as jnp
from jax import lax
from jax.experimental import pallas as pl
from jax.experimental.pallas import tpu as pltpu

F32 = jnp.float32
BF16 = jnp.bfloat16

GM_HEADS = 4
GM_HEAD_DIM = 64
GM_WIDTH = GM_HEADS * GM_HEAD_DIM
GM_CHUNK = 128
MLA_HEADS = 8
MLA_NOPE = 64
MLA_ROPE = 32
MLA_V = 64
MLA_WIDTH = MLA_HEADS * MLA_V
Q_LORA = 256
KV_LORA = 128
ROPE_BASE = 10000.0
SSM_GROUPS = 16
SSM_GROUP_CH = 16
SSM_WIDTH = SSM_GROUPS * SSM_GROUP_CH
SSM_STATE = 64
SSM_NSTATE = SSM_GROUPS * SSM_STATE
LN_EPS = 1e-5
RMS_EPS = 1e-6
NEG_BIG = -1e30

LANES = 128
BF16_SUBLANES = 16
HEAD_PAD = LANES
MLA_VP = MLA_V + BF16_SUBLANES
SOFTMAX_C = (MLA_NOPE + MLA_ROPE) ** -0.5 * math.log2(math.e)
VMEM_LIMIT = 56 * 1024 * 1024

ROW_TILE = 512
ATTN_TQ = 512
ATTN_TK = 512
ATTN_CHUNK = ATTN_TK
SSM_TL = 128
SSM_CHUNK = 32


def _gelu(x):
    c = math.sqrt(2.0 / math.pi)
    return 0.5 * x * (1.0 + jnp.tanh(c * (x + 0.044715 * (x * x * x))))


def _sigmoid(x):
    return 1.0 / (1.0 + jnp.exp(-x))


def _layer_norm(r, g, b):
    mu = jnp.mean(r, axis=-1, keepdims=True)
    d = r - mu
    var = jnp.mean(d * d, axis=-1, keepdims=True)
    return d * lax.rsqrt(var + LN_EPS) * g + b


def _rms_scale(y):
    return y * lax.rsqrt(jnp.mean(y * y, axis=-1, keepdims=True) + RMS_EPS)


def _dot(a, b):
    return jnp.dot(a, b, preferred_element_type=F32)


def _swiglu_ln(x, wg_ref, wu_ref, wd_ref, g, b, alpha, f_chunk):
    xb = x.astype(BF16)
    d_ff = wg_ref.shape[1]
    acts = []
    for c0 in range(0, d_ff, f_chunk):
        hg = _dot(xb, wg_ref[:, c0:c0 + f_chunk])
        hu = _dot(xb, wu_ref[:, c0:c0 + f_chunk])
        acts.append(((hg * _sigmoid(hg)) * hu).astype(BF16))
    half = x.shape[0] // 2
    outs = []
    for r0 in (0, half):
        y = None
        for ci, c0 in enumerate(range(0, d_ff, f_chunk)):
            part = _dot(acts[ci][r0:r0 + half, :], wd_ref[c0:c0 + f_chunk, :])
            y = part if y is None else y + part
        outs.append(_layer_norm(alpha * x[r0:r0 + half, :] + y, g, b))
    return jnp.concatenate(outs, axis=0)


def _ffn_ln_kernel(x_ref, wg_ref, wu_ref, wd_ref, g_ref, b_ref, o_ref, *, alpha, f_chunk):
    o_ref[...] = _swiglu_ln(x_ref[...], wg_ref, wu_ref, wd_ref, g_ref[...], b_ref[...], alpha, f_chunk)


def _resident(shape):
    nd = len(shape)
    return pl.BlockSpec(shape, lambda *_: (0,) * nd, pipeline_mode=pl.Buffered(1))


def _ffn_chunk(d_ff):
    return d_ff // 2 if (d_ff // 2) % LANES == 0 else d_ff


def _ffn_ln(x, wg, wu, wd, g, b, alpha):
    t, d = x.shape
    tm = ROW_TILE
    kern = functools.partial(_ffn_ln_kernel, alpha=alpha, f_chunk=_ffn_chunk(wg.shape[1]))
    return pl.pallas_call(
        kern,
        grid=(t // tm,),
        in_specs=[
            pl.BlockSpec((tm, d), lambda i: (i, 0)),
            _resident(wg.shape), _resident(wu.shape), _resident(wd.shape),
            _resident(g.shape), _resident(b.shape),
        ],
        out_specs=pl.BlockSpec((tm, d), lambda i: (i, 0)),
        out_shape=jax.ShapeDtypeStruct((t, d), F32),
        compiler_params=pltpu.CompilerParams(
            dimension_semantics=("arbitrary",), vmem_limit_bytes=VMEM_LIMIT),
        name="ffn_ln",
    )(x, wg, wu, wd, g, b)


def _rope_table_kernel(pos_ref, freq_ref, cos_ref, sin_ref):
    ang = pos_ref[...].astype(F32) * freq_ref[...]
    lane = lax.broadcasted_iota(jnp.int32, ang.shape, 1)
    half = MLA_ROPE // 2
    c = jnp.cos(ang)
    s = jnp.sin(ang)
    in_rope = (lane >= MLA_NOPE) & (lane < MLA_NOPE + MLA_ROPE)
    cos_ref[...] = jnp.where(lane < MLA_NOPE, 1.0, jnp.where(in_rope, c, 0.0))
    sin_ref[...] = jnp.where(in_rope, jnp.where(lane < MLA_NOPE + half, -s, s), 0.0)


def _rope_tables(positions):
    b, s = positions.shape
    t = b * s
    tm = ROW_TILE
    half = MLA_ROPE // 2
    inv_freq = 1.0 / (ROPE_BASE ** (jnp.arange(0, MLA_ROPE, 2, dtype=F32) / MLA_ROPE))
    freq_row = jnp.zeros((1, HEAD_PAD), F32)
    freq_row = freq_row.at[0, MLA_NOPE:MLA_NOPE + half].set(inv_freq)
    freq_row = freq_row.at[0, MLA_NOPE + half:MLA_NOPE + MLA_ROPE].set(inv_freq)
    pos = positions.reshape(t, 1)
    return pl.pallas_call(
        _rope_table_kernel,
        grid=(t // tm,),
        in_specs=[pl.BlockSpec((tm, 1), lambda i: (i, 0)),
                  pl.BlockSpec((1, HEAD_PAD), lambda i: (0, 0))],
        out_specs=[pl.BlockSpec((tm, HEAD_PAD), lambda i: (i, 0))] * 2,
        out_shape=[jax.ShapeDtypeStruct((t, HEAD_PAD), F32)] * 2,
        compiler_params=pltpu.CompilerParams(dimension_semantics=("arbitrary",)),
        name="rope_tables",
    )(pos, freq_row)


def _rope_tile(x, partner, cos_t, sin_t):
    return x * cos_t + partner * sin_t


def _in_proj_kernel(x_ref, win_ref, gng_ref, gws_ref, gbias_ref, qg_ref, wuq_ref, wuqp_ref, kvg_ref,
                    wuk_ref, wuvt_ref, vones_ref, cos_ref, sin_ref, gmix_a_ref,
                    ya_ref, q_ref, k_ref, vt_ref, u_ref):
    tm = x_ref.shape[0]
    xb = x_ref[...].astype(BF16)
    h = _dot(xb, win_ref[...])
    o_v = GM_WIDTH
    o_q = 2 * GM_WIDTH
    o_kv = o_q + Q_LORA
    o_kr = o_kv + KV_LORA
    o_krp = o_kr + HEAD_PAD
    o_ssm = o_krp + HEAD_PAD

    u_ref[...] = h[:, o_ssm:o_ssm + SSM_WIDTH]

    ug = _gelu(h[:, 0:GM_WIDTH])
    vg = _gelu(h[:, o_v:o_v + GM_WIDTH])
    lane_g = lax.broadcasted_iota(jnp.int32, (1, GM_WIDTH), 1)
    head_masks = [(lane_g >= hd * GM_HEAD_DIM) & (lane_g < (hd + 1) * GM_HEAD_DIM)
                  for hd in range(GM_HEADS)]

    def seg_mean(a):
        out = jnp.zeros_like(a)
        for m in head_masks:
            s = jnp.sum(jnp.where(m, a, 0.0), axis=-1, keepdims=True) * (1.0 / GM_HEAD_DIM)
            out = jnp.where(m, s, out)
        return out

    dv = vg - seg_mean(vg)
    vn = dv * lax.rsqrt(seg_mean(dv * dv) + LN_EPS) * gng_ref[...]
    vnb = vn.astype(BF16)
    row = lax.broadcasted_iota(jnp.int32, (GM_CHUNK, GM_CHUNK), 0)
    col = lax.broadcasted_iota(jnp.int32, (GM_CHUNK, GM_CHUNK), 1)
    tril = col <= row
    wc = [jnp.where(tril, gws_ref[hd], 0.0).astype(BF16) for hd in range(GM_HEADS)]
    gbias = gbias_ref[...]
    gmix_a = gmix_a_ref[...]
    for c0 in range(0, tm, GM_CHUNK):
        vc = vnb[c0:c0 + GM_CHUNK, :]
        z = jnp.zeros((GM_CHUNK, GM_WIDTH), F32)
        for hd in range(GM_HEADS):
            z = jnp.where(head_masks[hd], _dot(wc[hd], vc), z)
        ya = ug[c0:c0 + GM_CHUNK, :] * (z + gbias)
        ya_ref[c0:c0 + GM_CHUNK, :] = (_rms_scale(ya) * gmix_a).astype(BF16)

    cos_t = cos_ref[...]
    sin_t = sin_ref[...]
    cq = h[:, o_q:o_q + Q_LORA]
    cqn = (_rms_scale(cq) * qg_ref[...]).astype(BF16)
    q = _dot(cqn, wuq_ref[...])
    qp = _dot(cqn, wuqp_ref[...])
    for hd in range(MLA_HEADS):
        cols = slice(hd * HEAD_PAD, (hd + 1) * HEAD_PAD)
        q_ref[:, cols] = (_rope_tile(q[:, cols], qp[:, cols], cos_t, sin_t) * SOFTMAX_C).astype(BF16)
    ckv = h[:, o_kv:o_kv + KV_LORA]
    ckvn = (_rms_scale(ckv) * kvg_ref[...]).astype(BF16)
    kn = _dot(ckvn, wuk_ref[...])
    kr = _rope_tile(h[:, o_kr:o_kr + HEAD_PAD], h[:, o_krp:o_krp + HEAD_PAD], cos_t, sin_t)
    for hd in range(MLA_HEADS):
        k_ref[:, hd * HEAD_PAD:(hd + 1) * HEAD_PAD] = (kn[:, hd * HEAD_PAD:(hd + 1) * HEAD_PAD] + kr).astype(BF16)
    vt = lax.dot_general(wuvt_ref[...], ckvn, (((1,), (1,)), ((), ())),
                         preferred_element_type=F32)
    vt_ref[0, 0] = (vt + vones_ref[...]).astype(BF16)


def _in_proj(x, win, gng, gws, gbias, qg, wuq, wuqp, kvg, wuk, wuvt, cos_t, sin_t, gmix_a, batch, seq):
    t, d = x.shape
    tm = ROW_TILE
    spb = seq // tm
    row = lambda w: pl.BlockSpec((tm, w), lambda i: (i, 0))
    vrows = MLA_HEADS * MLA_VP
    is_one = (jnp.arange(vrows) % MLA_VP) == MLA_V
    vones = jnp.broadcast_to(is_one[:, None], (vrows, tm)).astype(F32)
    return pl.pallas_call(
        _in_proj_kernel,
        grid=(t // tm,),
        in_specs=[row(d), _resident(win.shape), _resident(gng.shape), _resident(gws.shape),
                  _resident(gbias.shape), _resident(qg.shape), _resident(wuq.shape),
                  _resident(wuqp.shape), _resident(kvg.shape), _resident(wuk.shape), _resident(wuvt.shape),
                  _resident(vones.shape), row(HEAD_PAD), row(HEAD_PAD), _resident(gmix_a.shape)],
        out_specs=[row(GM_WIDTH), row(MLA_HEADS * HEAD_PAD), row(MLA_HEADS * HEAD_PAD),
                   pl.BlockSpec((1, 1, vrows, tm), lambda i: (i // spb, i % spb, 0, 0)),
                   row(SSM_WIDTH)],
        out_shape=[jax.ShapeDtypeStruct((t, GM_WIDTH), BF16),
                   jax.ShapeDtypeStruct((t, MLA_HEADS * HEAD_PAD), BF16),
                   jax.ShapeDtypeStruct((t, MLA_HEADS * HEAD_PAD), BF16),
                   jax.ShapeDtypeStruct((batch, spb, vrows, tm), BF16),
                   jax.ShapeDtypeStruct((t, SSM_WIDTH), F32)],
        compiler_params=pltpu.CompilerParams(
            dimension_semantics=("arbitrary",), vmem_limit_bytes=VMEM_LIMIT),
        name="in_proj",
    )(x, win, gng, gws, gbias, qg, wuq, wuqp, kvg, wuk, wuvt, vones, cos_t, sin_t, gmix_a)


def _attn_kernel(q_ref, k_ref, vt_ref, o_ref, s0_ref, s1_ref, bm0_ref, bm1_ref,
                 m0_ref, m1_ref, acc0_ref, acc1_ref, *, tq, nq):
    tk = tq
    states = ((m0_ref, acc0_ref), (m1_ref, acc1_ref))

    def reset(st):
        m_ref, acc_ref = st
        m_ref[...] = jnp.full(m_ref.shape, NEG_BIG, F32)
        acc_ref[...] = jnp.zeros(acc_ref.shape, F32)

    def finalize(qb, st):
        _, acc_ref = st
        outs = []
        for hh in range(2):
            r0 = hh * MLA_VP
            inv_l = 1.0 / acc_ref[r0 + MLA_V:r0 + MLA_V + 1, :]
            outs.append(acc_ref[r0:r0 + MLA_V, :] * inv_l)
        o_ref[0, pl.ds(pl.multiple_of(qb * tq, tq), tq), :] = jnp.concatenate(outs, axis=0).T

    def stage(score, acc_from, st=None, done=None):
        masked = False
        score_to = None
        if acc_from is not None:
            m_ref, acc_ref = st
            ja, (sa_ref, bma_ref) = acc_from
            m_old = [m_ref[hh] for hh in range(2)]
            m_new = [jnp.maximum(m_old[hh], bma_ref[hh]) for hh in range(2)]
        if score is not None:
            qb_s, js, (ss_ref, bms_ref), masked = score
            score_to = score
            q0 = pl.multiple_of(qb_s * tq, tq)
            qs = [q_ref[0, pl.ds(q0, tq), hh * HEAD_PAD:(hh + 1) * HEAD_PAD] for hh in range(2)]
            start = pl.multiple_of(js * tk, tk)
            bmax = [None, None]
        order = [(kind, c0) for c0 in range(0, tk, ATTN_CHUNK) for kind in ("s", "a")]
        for kind, c0 in order:
            for hh in range(2):
                rows = slice(hh * MLA_VP, (hh + 1) * MLA_VP)
                if kind == "s" and score_to is not None:
                    kb = k_ref[0, pl.ds(start + c0, ATTN_CHUNK), hh * HEAD_PAD:(hh + 1) * HEAD_PAD]
                    s = lax.dot_general(kb, qs[hh], (((1,), (1,)), ((), ())),
                                        preferred_element_type=F32)
                    if masked:
                        krow = lax.broadcasted_iota(jnp.int32, s.shape, 0) + c0
                        qcol = lax.broadcasted_iota(jnp.int32, s.shape, 1)
                        s = jnp.where(krow <= qcol, s, NEG_BIG)
                    ss_ref[hh, c0:c0 + ATTN_CHUNK, :] = s
                    cmax = jnp.max(s, axis=0, keepdims=True)
                    bmax[hh] = cmax if bmax[hh] is None else jnp.maximum(bmax[hh], cmax)
                if kind == "a" and acc_from is not None:
                    p = jnp.exp2(sa_ref[hh, c0:c0 + ATTN_CHUNK, :] - m_new[hh]).astype(BF16)
                    d = _dot(vt_ref[0, ja, rows, c0:c0 + ATTN_CHUNK], p)
                    if c0 == 0:
                        acc_ref[rows, :] = jnp.exp2(m_old[hh] - m_new[hh]) * acc_ref[rows, :] + d
                    else:
                        acc_ref[rows, :] += d
        for hh in range(2):
            if acc_from is not None:
                m_ref[hh] = m_new[hh]
            if score_to is not None:
                bms_ref[hh] = bmax[hh]
        if done is not None:
            finalize(*done)
            reset(done[1])

    bufs = ((s0_ref, bm0_ref), (s1_ref, bm1_ref))

    diag_buf = (0, 1, 1, 0)
    reset(states[0])
    reset(states[1])
    stage((0, 0, bufs[0], True), None)

    def four_q_blocks(g, carry):
        for r in range(4):
            qb = 4 * g + r
            x = diag_buf[r]
            st = states[r % 2]

            def pair(t, c, qb=qb, x=x, st=st):
                stage((qb, 2 * t, bufs[x ^ 1], False), (jnp.where(t == 0, qb, 2 * t - 1), bufs[x]), st)
                stage((qb, 2 * t + 1, bufs[x], False), (2 * t, bufs[x ^ 1]), st)
                return c

            lax.fori_loop(0, qb // 2, pair, 0)
            if r % 2 == 1:
                stage((qb, qb - 1, bufs[x ^ 1], False), (jnp.where(qb == 1, qb, qb - 2), bufs[x]), st)
                last_buf = x ^ 1
            else:
                last_buf = x
            assert diag_buf[(r + 1) % 4] == last_buf ^ 1
            stage((jnp.minimum(qb + 1, nq - 1), jnp.minimum(qb + 1, nq - 1), bufs[last_buf ^ 1], True),
                  (jnp.where(qb == 0, qb, qb - 1), bufs[last_buf]), st,
                  done=(qb - 1, states[(r - 1) % 2]) if r > 0 else None)
            if r == 3:
                finalize(qb, st)
                reset(st)
        return carry

    lax.fori_loop(0, nq // 4, four_q_blocks, 0)


def _attention(q, k, vt, batch, seq):
    tq, tk = ATTN_TQ, ATTN_TK
    nq = seq // tq
    assert nq % 4 == 0
    q3 = q.reshape(batch, seq, MLA_HEADS * HEAD_PAD)
    k3 = k.reshape(batch, seq, MLA_HEADS * HEAD_PAD)
    kern = functools.partial(_attn_kernel, tq=tq, nq=nq)
    o = pl.pallas_call(
        kern,
        grid=(batch, MLA_HEADS // 2),
        in_specs=[pl.BlockSpec((1, seq, 2 * HEAD_PAD), lambda b, h: (b, 0, h)),
                  pl.BlockSpec((1, seq, 2 * HEAD_PAD), lambda b, h: (b, 0, h)),
                  pl.BlockSpec((1, seq // tk, 2 * MLA_VP, tk), lambda b, h: (b, 0, h, 0))],
        out_specs=pl.BlockSpec((1, seq, 2 * MLA_V), lambda b, h: (b, 0, h)),
        out_shape=jax.ShapeDtypeStruct((batch, seq, MLA_WIDTH), F32),
        scratch_shapes=[pltpu.VMEM((2, tk, tq), F32), pltpu.VMEM((2, tk, tq), F32),
                        pltpu.VMEM((2, 1, tq), F32), pltpu.VMEM((2, 1, tq), F32),
                        pltpu.VMEM((2, 1, tq), F32), pltpu.VMEM((2, 1, tq), F32),
                        pltpu.VMEM((2 * MLA_VP, tq), F32), pltpu.VMEM((2 * MLA_VP, tq), F32)],
        compiler_params=pltpu.CompilerParams(
            dimension_semantics=("arbitrary", "arbitrary"), vmem_limit_bytes=VMEM_LIMIT),
        name="attention",
    )(q3, k3, vt)
    return o.reshape(batch * seq, MLA_WIDTH)


def _s5_prep_kernel(ar_ref, ai_ref, dt_ref, bre_ref, bim_ref, abr_ref, abi_ref, bb_ref):
    ar = ar_ref[...]
    ai = ai_ref[...]
    dt = jnp.exp(dt_ref[...])
    mag = jnp.exp(ar * dt)
    abr = mag * jnp.cos(ai * dt)
    abi = mag * jnp.sin(ai * dt)
    den = ar * ar + ai * ai
    cr = ((abr - 1.0) * ar + abi * ai) / den
    ci = (abi * ar - (abr - 1.0) * ai) / den
    abr_ref[...] = jnp.broadcast_to(abr, abr_ref.shape)
    abi_ref[...] = jnp.broadcast_to(abi, abi_ref.shape)
    bre = bre_ref[...]
    bim = bim_ref[...]
    r = lax.broadcasted_iota(jnp.int32, bre.shape, 0) // SSM_GROUP_CH
    c = lax.broadcasted_iota(jnp.int32, bre.shape, 1) // SSM_STATE
    same = r == c
    bb_ref[:, 0:SSM_NSTATE] = jnp.where(same, cr * bre - ci * bim, 0.0).astype(BF16)
    bb_ref[:, SSM_NSTATE:2 * SSM_NSTATE] = jnp.where(same, cr * bim + ci * bre, 0.0).astype(BF16)


def _s5_prep(a_re, a_im, log_dt, b_re, b_im, batch):
    n = SSM_NSTATE
    ar = a_re.reshape(1, n)
    ai = a_im.reshape(1, n)
    dt = jnp.repeat(log_dt, SSM_STATE).reshape(1, n)
    def expand(bm):
        cols = jnp.transpose(bm, (2, 0, 1)).reshape(SSM_GROUP_CH, n)
        return jnp.tile(cols, (SSM_GROUPS, 1))
    whole = lambda shp: pl.BlockSpec(shp, lambda: (0,) * len(shp))
    return pl.pallas_call(
        _s5_prep_kernel,
        in_specs=[whole((1, n)), whole((1, n)), whole((1, n)),
                  whole((SSM_WIDTH, n)), whole((SSM_WIDTH, n))],
        out_specs=[whole((batch, n)), whole((batch, n)), whole((SSM_WIDTH, 2 * n))],
        out_shape=[jax.ShapeDtypeStruct((batch, n), F32), jax.ShapeDtypeStruct((batch, n), F32),
                   jax.ShapeDtypeStruct((SSM_WIDTH, 2 * n), BF16)],
        name="s5_prep",
    )(ar, ai, dt, expand(b_re), expand(b_im))


def _s5_kernel(u_ref, abr_ref, abi_ref, bb_ref, cc_ref, d_ref, gw_ref, gb_ref, gmix_ref,
               y_ref, hre_ref, him_ref, cre_ref, cim_ref, *, batch, tl, ch):
    n = SSM_NSTATE
    rc = ch * batch
    nch = tl // ch

    @pl.when(pl.program_id(0) == 0)
    def _():
        cre_ref[...] = jnp.zeros(cre_ref.shape, F32)
        cim_ref[...] = jnp.zeros(cim_ref.shape, F32)

    ar = abr_ref[...]
    ai = abi_ref[...]

    def expand(k):
        r = slice(k * rc, (k + 1) * rc)
        ub = u_ref[r, :].astype(BF16)
        hre_ref[r, :] = _dot(ub, bb_ref[:, 0:n])
        him_ref[r, :] = _dot(ub, bb_ref[:, n:2 * n])

    def scan(k, hr, hi):
        for t in range(ch):
            r = slice(k * rc + t * batch, k * rc + (t + 1) * batch)
            nr = ar * hr - ai * hi + hre_ref[r, :]
            ni = ar * hi + ai * hr + him_ref[r, :]
            hre_ref[r, :] = nr
            him_ref[r, :] = ni
            hr, hi = nr, ni
        return hr, hi

    def contract(k):
        r = slice(k * rc, (k + 1) * rc)
        y = (_dot(hre_ref[r, :].astype(BF16), cc_ref[0:n, :])
             + _dot(him_ref[r, :].astype(BF16), cc_ref[n:2 * n, :])
             + d_ref[...] * u_ref[r, :])
        y = _gelu(y)
        y = y * _sigmoid(_dot(y.astype(BF16), gw_ref[...]) + gb_ref[...])
        y_ref[r, :] = (_rms_scale(y) * gmix_ref[...]).astype(BF16)

    hr, hi = cre_ref[...], cim_ref[...]
    expand(0)
    for k in range(nch):
        if k + 1 < nch:
            expand(k + 1)
        hr, hi = scan(k, hr, hi)
        if k >= 1:
            contract(k - 1)
    contract(nch - 1)
    cre_ref[...] = hr
    cim_ref[...] = hi


def _s5(u_tm, abr, abi, bb, cc, dskip, gw, gb, gmix_c, batch):
    rows, w = u_tm.shape
    tl = SSM_TL
    tr = tl * batch
    n = SSM_NSTATE
    kern = functools.partial(_s5_kernel, batch=batch, tl=tl, ch=SSM_CHUNK)
    return pl.pallas_call(
        kern,
        grid=(rows // tr,),
        in_specs=[pl.BlockSpec((tr, w), lambda i: (i, 0)),
                  _resident(abr.shape), _resident(abi.shape), _resident(bb.shape),
                  _resident(cc.shape), _resident(dskip.shape), _resident(gw.shape),
                  _resident(gb.shape), _resident(gmix_c.shape)],
        out_specs=pl.BlockSpec((tr, w), lambda i: (i, 0)),
        out_shape=jax.ShapeDtypeStruct((rows, w), BF16),
        scratch_shapes=[pltpu.VMEM((tr, n), F32), pltpu.VMEM((tr, n), F32),
                        pltpu.VMEM((batch, n), F32), pltpu.VMEM((batch, n), F32)],
        compiler_params=pltpu.CompilerParams(
            dimension_semantics=("arbitrary",), vmem_limit_bytes=VMEM_LIMIT),
        name="s5",
    )(u_tm, abr, abi, bb, cc, dskip, gw, gb, gmix_c)


def _out_ffn_kernel(x_ref, ya_ref, o_ref, yc_ref, gmix_b_ref, wa_ref, wb_ref, wc_ref,
                    g1_ref, b1_ref, wg_ref, wu_ref, wd_ref, g2_ref, b2_ref, out_ref,
                    *, alpha, f_chunk):
    half = x_ref.shape[0] // 2
    parts = []
    for r0 in (0, half):
        r = slice(r0, r0 + half)
        yb = (_rms_scale(o_ref[r, :]) * gmix_b_ref[...]).astype(BF16)
        y = _dot(ya_ref[r, :], wa_ref[...]) + _dot(yb, wb_ref[...]) + _dot(yc_ref[r, :], wc_ref[...])
        parts.append(_layer_norm(alpha * x_ref[r, :] + y, g1_ref[...], b1_ref[...]))
    x1 = jnp.concatenate(parts, axis=0)
    out_ref[...] = _swiglu_ln(x1, wg_ref, wu_ref, wd_ref, g2_ref[...], b2_ref[...], alpha, f_chunk)


def _out_ffn(x, ya, o, yc, gmix_b, wa, wb, wc, g1, b1, wg, wu, wd, g2, b2, alpha):
    t, d = x.shape
    tm = ROW_TILE
    row = lambda w: pl.BlockSpec((tm, w), lambda i: (i, 0))
    kern = functools.partial(_out_ffn_kernel, alpha=alpha, f_chunk=_ffn_chunk(wg.shape[1]))
    return pl.pallas_call(
        kern,
        grid=(t // tm,),
        in_specs=[row(d), row(GM_WIDTH), row(MLA_WIDTH), row(SSM_WIDTH),
                  _resident(gmix_b.shape), _resident(wa.shape), _resident(wb.shape),
                  _resident(wc.shape), _resident(g1.shape), _resident(b1.shape),
                  _resident(wg.shape), _resident(wu.shape), _resident(wd.shape),
                  _resident(g2.shape), _resident(b2.shape)],
        out_specs=row(d),
        out_shape=jax.ShapeDtypeStruct((t, d), F32),
        compiler_params=pltpu.CompilerParams(
            dimension_semantics=("arbitrary",), vmem_limit_bytes=VMEM_LIMIT),
        name="out_ffn",
    )(x, ya, o, yc, gmix_b, wa, wb, wc, g1, b1, wg, wu, wd, g2, b2)


def _pack_w_in(w_in):
    d = w_in.shape[0]
    half = MLA_ROPE // 2
    o1 = 2 * GM_WIDTH
    o2 = o1 + Q_LORA
    o3 = o2 + KV_LORA
    o4 = o3 + MLA_ROPE
    zeros = jnp.zeros((d, HEAD_PAD), w_in.dtype)
    kr_tile = zeros.at[:, MLA_NOPE:MLA_NOPE + MLA_ROPE].set(w_in[:, o3:o4])
    swapped = jnp.concatenate([w_in[:, o3 + half:o4], w_in[:, o3:o3 + half]], axis=1)
    krp_tile = zeros.at[:, MLA_NOPE:MLA_NOPE + MLA_ROPE].set(swapped)
    return jnp.concatenate([w_in[:, :o3], kr_tile, krp_tile, w_in[:, o4:]], axis=1).astype(BF16)


def _pack_w_uq(w_uq):
    r = w_uq.shape[0]
    half = MLA_ROPE // 2
    w = w_uq.reshape(r, MLA_HEADS, MLA_NOPE + MLA_ROPE)
    pad = HEAD_PAD - MLA_NOPE - MLA_ROPE
    wq = jnp.pad(w, ((0, 0), (0, 0), (0, pad)))
    swapped = jnp.concatenate([w[:, :, MLA_NOPE + half:], w[:, :, MLA_NOPE:MLA_NOPE + half]], axis=2)
    wp = jnp.pad(swapped, ((0, 0), (0, 0), (MLA_NOPE, pad)))
    shape = (r, MLA_HEADS * HEAD_PAD)
    return wq.reshape(shape).astype(BF16), wp.reshape(shape).astype(BF16)


def _pack_w_ukv(w_ukv):
    r = w_ukv.shape[0]
    w = w_ukv.reshape(r, MLA_HEADS, MLA_NOPE + MLA_V)
    wk = jnp.pad(w[:, :, :MLA_NOPE], ((0, 0), (0, 0), (0, HEAD_PAD - MLA_NOPE)))
    wvt = jnp.pad(jnp.transpose(w[:, :, MLA_NOPE:], (1, 2, 0)), ((0, 0), (0, MLA_VP - MLA_V), (0, 0)))
    return (wk.reshape(r, MLA_HEADS * HEAD_PAD).astype(BF16),
            wvt.reshape(MLA_HEADS * MLA_VP, r).astype(BF16))


def _pack_ssm_c(c_re, c_im):
    def blockdiag(cm):
        eye = jnp.eye(SSM_GROUPS, dtype=cm.dtype)
        m = jnp.transpose(cm, (0, 2, 1))[:, :, None, :] * eye[:, None, :, None]
        return m.reshape(SSM_NSTATE, SSM_WIDTH)
    return jnp.concatenate([blockdiag(c_re), -blockdiag(c_im)], axis=0).astype(BF16)


def kernel(x, positions, ln_g, ln_b, ffn1_w_gate, ffn1_w_up, ffn1_w_down, w_in, gmlp_norm_g, gmlp_ws, gmlp_bs, mla_q_norm_g, mla_w_uq, mla_kv_norm_g, mla_w_ukv, ssm_a_re, ssm_a_im, ssm_b_re, ssm_b_im, ssm_c_re, ssm_c_im, ssm_d, ssm_log_dt, ssm_glu_w, ssm_glu_b, mix_norm_g, w_out, ffn2_w_gate, ffn2_w_up, ffn2_w_down):
    batch, seq, d = x.shape
    depth = w_in.shape[0]
    t = batch * seq
    assert batch == 8, "the S5 kernel keeps the batch on the 8 sublanes of a vreg"
    assert seq % max(ROW_TILE, ATTN_TQ, SSM_TL) == 0 and ROW_TILE % GM_CHUNK == 0
    assert ATTN_TQ == ATTN_TK == ROW_TILE
    alpha = (2 * depth) ** 0.25

    cos_t, sin_t = _rope_tables(positions)
    xf = x.reshape(t, d)
    row = lambda a: a.reshape(1, -1)
    for l in range(depth):
        xf = _ffn_ln(xf, ffn1_w_gate[l].astype(BF16), ffn1_w_up[l].astype(BF16),
                     (0.5 * ffn1_w_down[l]).astype(BF16), row(ln_g[l, 0]), row(ln_b[l, 0]), alpha)

        wk, wvt = _pack_w_ukv(mla_w_ukv[l])
        gbias = jnp.repeat(gmlp_bs[l].T, GM_HEAD_DIM, axis=1)
        gmix = mix_norm_g[l]
        ya, q, k, vt, u = _in_proj(
            xf, _pack_w_in(w_in[l]), row(gmlp_norm_g[l]), gmlp_ws[l], gbias,
            row(mla_q_norm_g[l]), *_pack_w_uq(mla_w_uq[l]), row(mla_kv_norm_g[l]), wk, wvt,
            cos_t, sin_t, row(gmix[:GM_WIDTH]), batch, seq)

        o = _attention(q, k, vt, batch, seq)

        abr, abi, bb = _s5_prep(ssm_a_re[l], ssm_a_im[l], ssm_log_dt[l], ssm_b_re[l], ssm_b_im[l], batch)
        u_tm = jnp.transpose(u.reshape(batch, seq, SSM_WIDTH), (1, 0, 2)).reshape(t, SSM_WIDTH)
        yc_tm = _s5(u_tm, abr, abi, bb, _pack_ssm_c(ssm_c_re[l], ssm_c_im[l]),
                    row(ssm_d[l]), ssm_glu_w[l].astype(BF16), row(ssm_glu_b[l]),
                    row(gmix[GM_WIDTH + MLA_WIDTH:]), batch)
        yc = jnp.transpose(yc_tm.reshape(seq, batch, SSM_WIDTH), (1, 0, 2)).reshape(t, SSM_WIDTH)

        wo = w_out[l].astype(BF16)
        xf = _out_ffn(xf, ya, o, yc, row(gmix[GM_WIDTH:GM_WIDTH + MLA_WIDTH]),
                      wo[:GM_WIDTH], wo[GM_WIDTH:GM_WIDTH + MLA_WIDTH], wo[GM_WIDTH + MLA_WIDTH:],
                      row(ln_g[l, 1]), row(ln_b[l, 1]),
                      ffn2_w_gate[l].astype(BF16), ffn2_w_up[l].astype(BF16),
                      (0.5 * ffn2_w_down[l]).astype(BF16), row(ln_g[l, 2]), row(ln_b[l, 2]), alpha)
    return xf.reshape(batch, seq, d)
```

```python
import functools
import math

import jax
import jax.numpy as jnp
from jax import lax
from jax.experimental import pallas as pl
from jax.experimental.pallas import tpu as pltpu

F32 = jnp.float32
BF16 = jnp.bfloat16

GM_HEADS = 4
GM_HEAD_DIM = 64
GM_WIDTH = GM_HEADS * GM_HEAD_DIM
GM_CHUNK = 128
MLA_HEADS = 8
MLA_NOPE = 64
MLA_ROPE = 32
MLA_V = 64
MLA_WIDTH = MLA_HEADS * MLA_V
Q_LORA = 256
KV_LORA = 128
ROPE_BASE = 10000.0
SSM_GROUPS = 16
SSM_GROUP_CH = 16
SSM_WIDTH = SSM_GROUPS * SSM_GROUP_CH
SSM_STATE = 64
SSM_NSTATE = SSM_GROUPS * SSM_STATE
LN_EPS = 1e-5
RMS_EPS = 1e-6
NEG_BIG = -1e30

LANES = 128
BF16_SUBLANES = 16
HEAD_PAD = LANES
MLA_VP = MLA_V + BF16_SUBLANES
SOFTMAX_C = (MLA_NOPE + MLA_ROPE) ** -0.5 * math.log2(math.e)
VMEM_LIMIT = 56 * 1024 * 1024

ROW_TILE = 512
ATTN_TQ = 512
ATTN_TK = 512
ATTN_CHUNK = ATTN_TK
SSM_TL = 128
SSM_CHUNK = 32


def _gelu(x):
    c = math.sqrt(2.0 / math.pi)
    return 0.5 * x * (1.0 + jnp.tanh(c * (x + 0.044715 * (x * x * x))))


def _sigmoid(x):
    return 1.0 / (1.0 + jnp.exp(-x))


def _layer_norm(r, g, b):
    mu = jnp.mean(r, axis=-1, keepdims=True)
    d = r - mu
    var = jnp.mean(d * d, axis=-1, keepdims=True)
    return d * lax.rsqrt(var + LN_EPS) * g + b


def _rms_scale(y):
    return y * lax.rsqrt(jnp.mean(y * y, axis=-1, keepdims=True) + RMS_EPS)


def _dot(a, b):
    return jnp.dot(a, b, preferred_element_type=F32)


def _swiglu_ln(x, wg_ref, wu_ref, wd_ref, g, b, alpha, f_chunk):
    xb = x.astype(BF16)
    d_ff = wg_ref.shape[1]
    acts = []
    for c0 in range(0, d_ff, f_chunk):
        hg = _dot(xb, wg_ref[:, c0:c0 + f_chunk])
        hu = _dot(xb, wu_ref[:, c0:c0 + f_chunk])
        acts.append(((hg * _sigmoid(hg)) * hu).astype(BF16))
    half = x.shape[0] // 2
    outs = []
    for r0 in (0, half):
        y = None
        for ci, c0 in enumerate(range(0, d_ff, f_chunk)):
            part = _dot(acts[ci][r0:r0 + half, :], wd_ref[c0:c0 + f_chunk, :])
            y = part if y is None else y + part
        outs.append(_layer_norm(alpha * x[r0:r0 + half, :] + y, g, b))
    return jnp.concatenate(outs, axis=0)


def _ffn_ln_kernel(x_ref, wg_ref, wu_ref, wd_ref, g_ref, b_ref, o_ref, *, alpha, f_chunk):
    o_ref[...] = _swiglu_ln(x_ref[...], wg_ref, wu_ref, wd_ref, g_ref[...], b_ref[...], alpha, f_chunk)


def _resident(shape):
    nd = len(shape)
    return pl.BlockSpec(shape, lambda *_: (0,) * nd, pipeline_mode=pl.Buffered(1))


def _ffn_chunk(d_ff):
    return d_ff // 2 if (d_ff // 2) % LANES == 0 else d_ff


def _ffn_ln(x, wg, wu, wd, g, b, alpha):
    t, d = x.shape
    tm = ROW_TILE
    kern = functools.partial(_ffn_ln_kernel, alpha=alpha, f_chunk=_ffn_chunk(wg.shape[1]))
    return pl.pallas_call(
        kern,
        grid=(t // tm,),
        in_specs=[
            pl.BlockSpec((tm, d), lambda i: (i, 0)),
            _resident(wg.shape), _resident(wu.shape), _resident(wd.shape),
            _resident(g.shape), _resident(b.shape),
        ],
        out_specs=pl.BlockSpec((tm, d), lambda i: (i, 0)),
        out_shape=jax.ShapeDtypeStruct((t, d), F32),
        compiler_params=pltpu.CompilerParams(
            dimension_semantics=("arbitrary",), vmem_limit_bytes=VMEM_LIMIT),
        name="ffn_ln",
    )(x, wg, wu, wd, g, b)


def _rope_table_kernel(pos_ref, freq_ref, cos_ref, sin_ref):
    ang = pos_ref[...].astype(F32) * freq_ref[...]
    lane = lax.broadcasted_iota(jnp.int32, ang.shape, 1)
    half = MLA_ROPE // 2
    c = jnp.cos(ang)
    s = jnp.sin(ang)
    in_rope = (lane >= MLA_NOPE) & (lane < MLA_NOPE + MLA_ROPE)
    cos_ref[...] = jnp.where(lane < MLA_NOPE, 1.0, jnp.where(in_rope, c, 0.0))
    sin_ref[...] = jnp.where(in_rope, jnp.where(lane < MLA_NOPE + half, -s, s), 0.0)


def _rope_tables(positions):
    b, s = positions.shape
    t = b * s
    tm = ROW_TILE
    half = MLA_ROPE // 2
    inv_freq = 1.0 / (ROPE_BASE ** (jnp.arange(0, MLA_ROPE, 2, dtype=F32) / MLA_ROPE))
    freq_row = jnp.zeros((1, HEAD_PAD), F32)
    freq_row = freq_row.at[0, MLA_NOPE:MLA_NOPE + half].set(inv_freq)
    freq_row = freq_row.at[0, MLA_NOPE + half:MLA_NOPE + MLA_ROPE].set(inv_freq)
    pos = positions.reshape(t, 1)
    return pl.pallas_call(
        _rope_table_kernel,
        grid=(t // tm,),
        in_specs=[pl.BlockSpec((tm, 1), lambda i: (i, 0)),
                  pl.BlockSpec((1, HEAD_PAD), lambda i: (0, 0))],
        out_specs=[pl.BlockSpec((tm, HEAD_PAD), lambda i: (i, 0))] * 2,
        out_shape=[jax.ShapeDtypeStruct((t, HEAD_PAD), F32)] * 2,
        compiler_params=pltpu.CompilerParams(dimension_semantics=("arbitrary",)),
        name="rope_tables",
    )(pos, freq_row)


def _rope_tile(x, partner, cos_t, sin_t):
    return x * cos_t + partner * sin_t


def _in_proj_kernel(x_ref, win_ref, gng_ref, gws_ref, gbias_ref, qg_ref, wuq_ref, wuqp_ref, kvg_ref,
                    wuk_ref, wuvt_ref, vones_ref, cos_ref, sin_ref, gmix_a_ref,
                    ya_ref, qt_ref, k_ref, vt_ref, u_ref):
    tm = x_ref.shape[0]
    xb = x_ref[...].astype(BF16)
    h = _dot(xb, win_ref[...])
    o_v = GM_WIDTH
    o_q = 2 * GM_WIDTH
    o_kv = o_q + Q_LORA
    o_kr = o_kv + KV_LORA
    o_krp = o_kr + HEAD_PAD
    o_ssm = o_krp + HEAD_PAD

    u_ref[...] = h[:, o_ssm:o_ssm + SSM_WIDTH]

    ug = _gelu(h[:, 0:GM_WIDTH])
    vg = _gelu(h[:, o_v:o_v + GM_WIDTH])
    lane_g = lax.broadcasted_iota(jnp.int32, (1, GM_WIDTH), 1)
    head_masks = [(lane_g >= hd * GM_HEAD_DIM) & (lane_g < (hd + 1) * GM_HEAD_DIM)
                  for hd in range(GM_HEADS)]

    def seg_mean(a):
        out = jnp.zeros_like(a)
        for m in head_masks:
            s = jnp.sum(jnp.where(m, a, 0.0), axis=-1, keepdims=True) * (1.0 / GM_HEAD_DIM)
            out = jnp.where(m, s, out)
        return out

    dv = vg - seg_mean(vg)
    vn = dv * lax.rsqrt(seg_mean(dv * dv) + LN_EPS) * gng_ref[...]
    vnb = vn.astype(BF16)
    row = lax.broadcasted_iota(jnp.int32, (GM_CHUNK, GM_CHUNK), 0)
    col = lax.broadcasted_iota(jnp.int32, (GM_CHUNK, GM_CHUNK), 1)
    tril = col <= row
    wc = [jnp.where(tril, gws_ref[hd], 0.0).astype(BF16) for hd in range(GM_HEADS)]
    gbias = gbias_ref[...]
    gmix_a = gmix_a_ref[...]
    for c0 in range(0, tm, GM_CHUNK):
        vc = vnb[c0:c0 + GM_CHUNK, :]
        z = jnp.zeros((GM_CHUNK, GM_WIDTH), F32)
        for hd in range(GM_HEADS):
            z = jnp.where(head_masks[hd], _dot(wc[hd], vc), z)
        ya = ug[c0:c0 + GM_CHUNK, :] * (z + gbias)
        ya_ref[c0:c0 + GM_CHUNK, :] = (_rms_scale(ya) * gmix_a).astype(BF16)

    cos_t = cos_ref[...]
    sin_t = sin_ref[...]
    cq = h[:, o_q:o_q + Q_LORA]
    cqn = (_rms_scale(cq) * qg_ref[...]).astype(BF16)
    nt = (((1,), (1,)), ((), ()))
    qt = lax.dot_general(wuq_ref[...], cqn, nt, preferred_element_type=F32)
    qpt = lax.dot_general(wuqp_ref[...], cqn, nt, preferred_element_type=F32)
    cos_tt = cos_t.T
    sin_tt = sin_t.T
    for hd in range(MLA_HEADS):
        rows = slice(hd * HEAD_PAD, (hd + 1) * HEAD_PAD)
        qt_ref[0, 0, rows, :] = (_rope_tile(qt[rows, :], qpt[rows, :], cos_tt, sin_tt)
                                 * SOFTMAX_C).astype(BF16)
    ckv = h[:, o_kv:o_kv + KV_LORA]
    ckvn = (_rms_scale(ckv) * kvg_ref[...]).astype(BF16)
    kn = _dot(ckvn, wuk_ref[...])
    kr = _rope_tile(h[:, o_kr:o_kr + HEAD_PAD], h[:, o_krp:o_krp + HEAD_PAD], cos_t, sin_t)
    for hd in range(MLA_HEADS):
        k_ref[:, hd * HEAD_PAD:(hd + 1) * HEAD_PAD] = (kn[:, hd * HEAD_PAD:(hd + 1) * HEAD_PAD] + kr).astype(BF16)
    vt = lax.dot_general(wuvt_ref[...], ckvn, (((1,), (1,)), ((), ())),
                         preferred_element_type=F32)
    vt_ref[0, 0] = (vt + vones_ref[...]).astype(BF16)


def _in_proj(x, win, gng, gws, gbias, qg, wuq, wuqp, kvg, wuk, wuvt, cos_t, sin_t, gmix_a, batch, seq):
    t, d = x.shape
    tm = ROW_TILE
    spb = seq // tm
    row = lambda w: pl.BlockSpec((tm, w), lambda i: (i, 0))
    vrows = MLA_HEADS * MLA_VP
    is_one = (jnp.arange(vrows) % MLA_VP) == MLA_V
    vones = jnp.broadcast_to(is_one[:, None], (vrows, tm)).astype(F32)
    return pl.pallas_call(
        _in_proj_kernel,
        grid=(t // tm,),
        in_specs=[row(d), _resident(win.shape), _resident(gng.shape), _resident(gws.shape),
                  _resident(gbias.shape), _resident(qg.shape), _resident(wuq.shape),
                  _resident(wuqp.shape), _resident(kvg.shape), _resident(wuk.shape), _resident(wuvt.shape),
                  _resident(vones.shape), row(HEAD_PAD), row(HEAD_PAD), _resident(gmix_a.shape)],
        out_specs=[row(GM_WIDTH),
                   pl.BlockSpec((1, 1, MLA_HEADS * HEAD_PAD, tm), lambda i: (i // spb, i % spb, 0, 0)),
                   row(MLA_HEADS * HEAD_PAD),
                   pl.BlockSpec((1, 1, vrows, tm), lambda i: (i // spb, i % spb, 0, 0)),
                   row(SSM_WIDTH)],
        out_shape=[jax.ShapeDtypeStruct((t, GM_WIDTH), BF16),
                   jax.ShapeDtypeStruct((batch, spb, MLA_HEADS * HEAD_PAD, tm), BF16),
                   jax.ShapeDtypeStruct((t, MLA_HEADS * HEAD_PAD), BF16),
                   jax.ShapeDtypeStruct((batch, spb, vrows, tm), BF16),
                   jax.ShapeDtypeStruct((t, SSM_WIDTH), F32)],
        compiler_params=pltpu.CompilerParams(
            dimension_semantics=("arbitrary",), vmem_limit_bytes=VMEM_LIMIT),
        name="in_proj",
    )(x, win, gng, gws, gbias, qg, wuq, wuqp, kvg, wuk, wuvt, vones, cos_t, sin_t, gmix_a)


def _attn_kernel(q_ref, k_ref, vt_ref, o_ref, s0_ref, s1_ref, bm0_ref, bm1_ref,
                 m0_ref, m1_ref, acc0_ref, acc1_ref, *, tq, nq):
    tk = tq
    states = ((m0_ref, acc0_ref), (m1_ref, acc1_ref))

    def reset(st):
        m_ref, acc_ref = st
        m_ref[...] = jnp.full(m_ref.shape, NEG_BIG, F32)
        acc_ref[...] = jnp.zeros(acc_ref.shape, F32)

    def finalize(qb, st):
        _, acc_ref = st
        outs = []
        for hh in range(2):
            r0 = hh * MLA_VP
            inv_l = 1.0 / acc_ref[r0 + MLA_V:r0 + MLA_V + 1, :]
            outs.append(acc_ref[r0:r0 + MLA_V, :] * inv_l)
        o_ref[0, pl.ds(pl.multiple_of(qb * tq, tq), tq), :] = jnp.concatenate(outs, axis=0).T

    def stage(score, acc_from, st=None, done=None):
        masked = False
        score_to = None
        if acc_from is not None:
            m_ref, acc_ref = st
            ja, (sa_ref, bma_ref) = acc_from
            m_old = [m_ref[hh] for hh in range(2)]
            m_new = [jnp.maximum(m_old[hh], bma_ref[hh]) for hh in range(2)]
        if score is not None:
            qb_s, js, (ss_ref, bms_ref), masked = score
            score_to = score
            qs = [q_ref[0, qb_s, hh * HEAD_PAD:(hh + 1) * HEAD_PAD, :] for hh in range(2)]
            start = pl.multiple_of(js * tk, tk)
            bmax = [None, None]
        order = [(kind, c0) for c0 in range(0, tk, ATTN_CHUNK) for kind in ("s", "a")]
        for kind, c0 in order:
            for hh in range(2):
                rows = slice(hh * MLA_VP, (hh + 1) * MLA_VP)
                if kind == "s" and score_to is not None:
                    kb = k_ref[0, pl.ds(start + c0, ATTN_CHUNK), hh * HEAD_PAD:(hh + 1) * HEAD_PAD]
                    s = _dot(kb, qs[hh])
                    if masked:
                        krow = lax.broadcasted_iota(jnp.int32, s.shape, 0) + c0
                        qcol = lax.broadcasted_iota(jnp.int32, s.shape, 1)
                        s = jnp.where(krow <= qcol, s, NEG_BIG)
                    ss_ref[hh, c0:c0 + ATTN_CHUNK, :] = s
                    cmax = jnp.max(s, axis=0, keepdims=True)
                    bmax[hh] = cmax if bmax[hh] is None else jnp.maximum(bmax[hh], cmax)
                if kind == "a" and acc_from is not None:
                    p = jnp.exp2(sa_ref[hh, c0:c0 + ATTN_CHUNK, :] - m_new[hh]).astype(BF16)
                    d = _dot(vt_ref[0, ja, rows, c0:c0 + ATTN_CHUNK], p)
                    if c0 == 0:
                        acc_ref[rows, :] = jnp.exp2(m_old[hh] - m_new[hh]) * acc_ref[rows, :] + d
                    else:
                        acc_ref[rows, :] += d
        for hh in range(2):
            if acc_from is not None:
                m_ref[hh] = m_new[hh]
            if score_to is not None:
                bms_ref[hh] = bmax[hh]
        if done is not None:
            finalize(*done)
            reset(done[1])

    bufs = ((s0_ref, bm0_ref), (s1_ref, bm1_ref))

    diag_buf = (0, 1, 1, 0)
    reset(states[0])
    reset(states[1])
    stage((0, 0, bufs[0], True), None)

    def four_q_blocks(g, carry):
        for r in range(4):
            qb = 4 * g + r
            x = diag_buf[r]
            st = states[r % 2]

            def pair(t, c, qb=qb, x=x, st=st):
                stage((qb, 2 * t, bufs[x ^ 1], False), (jnp.where(t == 0, qb, 2 * t - 1), bufs[x]), st)
                stage((qb, 2 * t + 1, bufs[x], False), (2 * t, bufs[x ^ 1]), st)
                return c

            lax.fori_loop(0, qb // 2, pair, 0)
            if r % 2 == 1:
                stage((qb, qb - 1, bufs[x ^ 1], False), (jnp.where(qb == 1, qb, qb - 2), bufs[x]), st)
                last_buf = x ^ 1
            else:
                last_buf = x
            assert diag_buf[(r + 1) % 4] == last_buf ^ 1
            stage((jnp.minimum(qb + 1, nq - 1), jnp.minimum(qb + 1, nq - 1), bufs[last_buf ^ 1], True),
                  (jnp.where(qb == 0, qb, qb - 1), bufs[last_buf]), st,
                  done=(qb - 1, states[(r - 1) % 2]) if r > 0 else None)
            if r == 3:
                finalize(qb, st)
                reset(st)
        return carry

    lax.fori_loop(0, nq // 4, four_q_blocks, 0)


def _attention(qt, k, vt, batch, seq):
    tq, tk = ATTN_TQ, ATTN_TK
    nq = seq // tq
    assert nq % 4 == 0
    k3 = k.reshape(batch, seq, MLA_HEADS * HEAD_PAD)
    kern = functools.partial(_attn_kernel, tq=tq, nq=nq)
    o = pl.pallas_call(
        kern,
        grid=(batch, MLA_HEADS // 2),
        in_specs=[pl.BlockSpec((1, nq, 2 * HEAD_PAD, tq), lambda b, h: (b, 0, h, 0)),
                  pl.BlockSpec((1, seq, 2 * HEAD_PAD), lambda b, h: (b, 0, h)),
                  pl.BlockSpec((1, seq // tk, 2 * MLA_VP, tk), lambda b, h: (b, 0, h, 0))],
        out_specs=pl.BlockSpec((1, seq, 2 * MLA_V), lambda b, h: (b, 0, h)),
        out_shape=jax.ShapeDtypeStruct((batch, seq, MLA_WIDTH), F32),
        scratch_shapes=[pltpu.VMEM((2, tk, tq), F32), pltpu.VMEM((2, tk, tq), F32),
                        pltpu.VMEM((2, 1, tq), F32), pltpu.VMEM((2, 1, tq), F32),
                        pltpu.VMEM((2, 1, tq), F32), pltpu.VMEM((2, 1, tq), F32),
                        pltpu.VMEM((2 * MLA_VP, tq), F32), pltpu.VMEM((2 * MLA_VP, tq), F32)],
        compiler_params=pltpu.CompilerParams(
            dimension_semantics=("arbitrary", "arbitrary"), vmem_limit_bytes=VMEM_LIMIT),
        name="attention",
    )(qt, k3, vt)
    return o.reshape(batch * seq, MLA_WIDTH)


def _s5_prep_kernel(ar_ref, ai_ref, dt_ref, bre_ref, bim_ref, abr_ref, abi_ref, bb_ref):
    ar = ar_ref[...]
    ai = ai_ref[...]
    dt = jnp.exp(dt_ref[...])
    mag = jnp.exp(ar * dt)
    abr = mag * jnp.cos(ai * dt)
    abi = mag * jnp.sin(ai * dt)
    den = ar * ar + ai * ai
    cr = ((abr - 1.0) * ar + abi * ai) / den
    ci = (abi * ar - (abr - 1.0) * ai) / den
    abr_ref[...] = jnp.broadcast_to(abr, abr_ref.shape)
    abi_ref[...] = jnp.broadcast_to(abi, abi_ref.shape)
    bre = bre_ref[...]
    bim = bim_ref[...]
    r = lax.broadcasted_iota(jnp.int32, bre.shape, 0) // SSM_GROUP_CH
    c = lax.broadcasted_iota(jnp.int32, bre.shape, 1) // SSM_STATE
    same = r == c
    bb_ref[:, 0:SSM_NSTATE] = jnp.where(same, cr * bre - ci * bim, 0.0).astype(BF16)
    bb_ref[:, SSM_NSTATE:2 * SSM_NSTATE] = jnp.where(same, cr * bim + ci * bre, 0.0).astype(BF16)


def _s5_prep(a_re, a_im, log_dt, b_re, b_im, batch):
    n = SSM_NSTATE
    ar = a_re.reshape(1, n)
    ai = a_im.reshape(1, n)
    dt = jnp.repeat(log_dt, SSM_STATE).reshape(1, n)
    def expand(bm):
        cols = jnp.transpose(bm, (2, 0, 1)).reshape(SSM_GROUP_CH, n)
        return jnp.tile(cols, (SSM_GROUPS, 1))
    whole = lambda shp: pl.BlockSpec(shp, lambda: (0,) * len(shp))
    return pl.pallas_call(
        _s5_prep_kernel,
        in_specs=[whole((1, n)), whole((1, n)), whole((1, n)),
                  whole((SSM_WIDTH, n)), whole((SSM_WIDTH, n))],
        out_specs=[whole((batch, n)), whole((batch, n)), whole((SSM_WIDTH, 2 * n))],
        out_shape=[jax.ShapeDtypeStruct((batch, n), F32), jax.ShapeDtypeStruct((batch, n), F32),
                   jax.ShapeDtypeStruct((SSM_WIDTH, 2 * n), BF16)],
        name="s5_prep",
    )(ar, ai, dt, expand(b_re), expand(b_im))


def _s5_kernel(u_ref, abr_ref, abi_ref, bb_ref, cc_ref, d_ref, gw_ref, gb_ref, gmix_ref,
               y_ref, hre_ref, him_ref, cre_ref, cim_ref, *, batch, tl, ch):
    n = SSM_NSTATE
    rc = ch * batch
    nch = tl // ch

    @pl.when(pl.program_id(0) == 0)
    def _():
        cre_ref[...] = jnp.zeros(cre_ref.shape, F32)
        cim_ref[...] = jnp.zeros(cim_ref.shape, F32)

    ar = abr_ref[...]
    ai = abi_ref[...]

    def expand(k):
        r = slice(k * rc, (k + 1) * rc)
        ub = u_ref[r, :].astype(BF16)
        hre_ref[r, :] = _dot(ub, bb_ref[:, 0:n])
        him_ref[r, :] = _dot(ub, bb_ref[:, n:2 * n])

    def scan(k, hr, hi):
        for t in range(ch):
            r = slice(k * rc + t * batch, k * rc + (t + 1) * batch)
            nr = ar * hr - ai * hi + hre_ref[r, :]
            ni = ar * hi + ai * hr + him_ref[r, :]
            hre_ref[r, :] = nr
            him_ref[r, :] = ni
            hr, hi = nr, ni
        return hr, hi

    def contract(k):
        r = slice(k * rc, (k + 1) * rc)
        y = (_dot(hre_ref[r, :].astype(BF16), cc_ref[0:n, :])
             + _dot(him_ref[r, :].astype(BF16), cc_ref[n:2 * n, :])
             + d_ref[...] * u_ref[r, :])
        y = _gelu(y)
        y = y * _sigmoid(_dot(y.astype(BF16), gw_ref[...]) + gb_ref[...])
        y_ref[r, :] = (_rms_scale(y) * gmix_ref[...]).astype(BF16)

    hr, hi = cre_ref[...], cim_ref[...]
    expand(0)
    for k in range(nch):
        if k + 1 < nch:
            expand(k + 1)
        hr, hi = scan(k, hr, hi)
        if k >= 1:
            contract(k - 1)
    contract(nch - 1)
    cre_ref[...] = hr
    cim_ref[...] = hi


def _s5(u_tm, abr, abi, bb, cc, dskip, gw, gb, gmix_c, batch):
    rows, w = u_tm.shape
    tl = SSM_TL
    tr = tl * batch
    n = SSM_NSTATE
    kern = functools.partial(_s5_kernel, batch=batch, tl=tl, ch=SSM_CHUNK)
    return pl.pallas_call(
        kern,
        grid=(rows // tr,),
        in_specs=[pl.BlockSpec((tr, w), lambda i: (i, 0)),
                  _resident(abr.shape), _resident(abi.shape), _resident(bb.shape),
                  _resident(cc.shape), _resident(dskip.shape), _resident(gw.shape),
                  _resident(gb.shape), _resident(gmix_c.shape)],
        out_specs=pl.BlockSpec((tr, w), lambda i: (i, 0)),
        out_shape=jax.ShapeDtypeStruct((rows, w), BF16),
        scratch_shapes=[pltpu.VMEM((tr, n), F32), pltpu.VMEM((tr, n), F32),
                        pltpu.VMEM((batch, n), F32), pltpu.VMEM((batch, n), F32)],
        compiler_params=pltpu.CompilerParams(
            dimension_semantics=("arbitrary",), vmem_limit_bytes=VMEM_LIMIT),
        name="s5",
    )(u_tm, abr, abi, bb, cc, dskip, gw, gb, gmix_c)


def _out_ffn_kernel(x_ref, ya_ref, o_ref, yc_ref, gmix_b_ref, wa_ref, wb_ref, wc_ref,
                    g1_ref, b1_ref, wg_ref, wu_ref, wd_ref, g2_ref, b2_ref, out_ref,
                    *, alpha, f_chunk):
    half = x_ref.shape[0] // 2
    parts = []
    for r0 in (0, half):
        r = slice(r0, r0 + half)
        yb = (_rms_scale(o_ref[r, :]) * gmix_b_ref[...]).astype(BF16)
        y = _dot(ya_ref[r, :], wa_ref[...]) + _dot(yb, wb_ref[...]) + _dot(yc_ref[r, :], wc_ref[...])
        parts.append(_layer_norm(alpha * x_ref[r, :] + y, g1_ref[...], b1_ref[...]))
    x1 = jnp.concatenate(parts, axis=0)
    out_ref[...] = _swiglu_ln(x1, wg_ref, wu_ref, wd_ref, g2_ref[...], b2_ref[...], alpha, f_chunk)


def _out_ffn(x, ya, o, yc, gmix_b, wa, wb, wc, g1, b1, wg, wu, wd, g2, b2, alpha):
    t, d = x.shape
    tm = ROW_TILE
    row = lambda w: pl.BlockSpec((tm, w), lambda i: (i, 0))
    kern = functools.partial(_out_ffn_kernel, alpha=alpha, f_chunk=_ffn_chunk(wg.shape[1]))
    return pl.pallas_call(
        kern,
        grid=(t // tm,),
        in_specs=[row(d), row(GM_WIDTH), row(MLA_WIDTH), row(SSM_WIDTH),
                  _resident(gmix_b.shape), _resident(wa.shape), _resident(wb.shape),
                  _resident(wc.shape), _resident(g1.shape), _resident(b1.shape),
                  _resident(wg.shape), _resident(wu.shape), _resident(wd.shape),
                  _resident(g2.shape), _resident(b2.shape)],
        out_specs=row(d),
        out_shape=jax.ShapeDtypeStruct((t, d), F32),
        compiler_params=pltpu.CompilerParams(
            dimension_semantics=("arbitrary",), vmem_limit_bytes=VMEM_LIMIT),
        name="out_ffn",
    )(x, ya, o, yc, gmix_b, wa, wb, wc, g1, b1, wg, wu, wd, g2, b2)


def _pack_w_in(w_in):
    d = w_in.shape[0]
    half = MLA_ROPE // 2
    o1 = 2 * GM_WIDTH
    o2 = o1 + Q_LORA
    o3 = o2 + KV_LORA
    o4 = o3 + MLA_ROPE
    zeros = jnp.zeros((d, HEAD_PAD), w_in.dtype)
    kr_tile = zeros.at[:, MLA_NOPE:MLA_NOPE + MLA_ROPE].set(w_in[:, o3:o4])
    swapped = jnp.concatenate([w_in[:, o3 + half:o4], w_in[:, o3:o3 + half]], axis=1)
    krp_tile = zeros.at[:, MLA_NOPE:MLA_NOPE + MLA_ROPE].set(swapped)
    return jnp.concatenate([w_in[:, :o3], kr_tile, krp_tile, w_in[:, o4:]], axis=1).astype(BF16)


def _pack_w_uq(w_uq):
    r = w_uq.shape[0]
    half = MLA_ROPE // 2
    w = w_uq.reshape(r, MLA_HEADS, MLA_NOPE + MLA_ROPE)
    pad = HEAD_PAD - MLA_NOPE - MLA_ROPE
    wq = jnp.pad(w, ((0, 0), (0, 0), (0, pad)))
    swapped = jnp.concatenate([w[:, :, MLA_NOPE + half:], w[:, :, MLA_NOPE:MLA_NOPE + half]], axis=2)
    wp = jnp.pad(swapped, ((0, 0), (0, 0), (MLA_NOPE, pad)))
    shape = (r, MLA_HEADS * HEAD_PAD)
    return wq.reshape(shape).T.astype(BF16), wp.reshape(shape).T.astype(BF16)


def _pack_w_ukv(w_ukv):
    r = w_ukv.shape[0]
    w = w_ukv.reshape(r, MLA_HEADS, MLA_NOPE + MLA_V)
    wk = jnp.pad(w[:, :, :MLA_NOPE], ((0, 0), (0, 0), (0, HEAD_PAD - MLA_NOPE)))
    wvt = jnp.pad(jnp.transpose(w[:, :, MLA_NOPE:], (1, 2, 0)), ((0, 0), (0, MLA_VP - MLA_V), (0, 0)))
    return (wk.reshape(r, MLA_HEADS * HEAD_PAD).astype(BF16),
            wvt.reshape(MLA_HEADS * MLA_VP, r).astype(BF16))


def _pack_ssm_c(c_re, c_im):
    def blockdiag(cm):
        eye = jnp.eye(SSM_GROUPS, dtype=cm.dtype)
        m = jnp.transpose(cm, (0, 2, 1))[:, :, None, :] * eye[:, None, :, None]
        return m.reshape(SSM_NSTATE, SSM_WIDTH)
    return jnp.concatenate([blockdiag(c_re), -blockdiag(c_im)], axis=0).astype(BF16)


def kernel(x, positions, ln_g, ln_b, ffn1_w_gate, ffn1_w_up, ffn1_w_down, w_in, gmlp_norm_g, gmlp_ws, gmlp_bs, mla_q_norm_g, mla_w_uq, mla_kv_norm_g, mla_w_ukv, ssm_a_re, ssm_a_im, ssm_b_re, ssm_b_im, ssm_c_re, ssm_c_im, ssm_d, ssm_log_dt, ssm_glu_w, ssm_glu_b, mix_norm_g, w_out, ffn2_w_gate, ffn2_w_up, ffn2_w_down):
    batch, seq, d = x.shape
    depth = w_in.shape[0]
    t = batch * seq
    assert batch == 8, "the S5 kernel keeps the batch on the 8 sublanes of a vreg"
    assert seq % max(ROW_TILE, ATTN_TQ, SSM_TL) == 0 and ROW_TILE % GM_CHUNK == 0
    assert ATTN_TQ == ATTN_TK == ROW_TILE
    alpha = (2 * depth) ** 0.25

    cos_t, sin_t = _rope_tables(positions)
    xf = x.reshape(t, d)
    row = lambda a: a.reshape(1, -1)
    for l in range(depth):
        xf = _ffn_ln(xf, ffn1_w_gate[l].astype(BF16), ffn1_w_up[l].astype(BF16),
                     (0.5 * ffn1_w_down[l]).astype(BF16), row(ln_g[l, 0]), row(ln_b[l, 0]), alpha)

        wk, wvt = _pack_w_ukv(mla_w_ukv[l])
        gbias = jnp.repeat(gmlp_bs[l].T, GM_HEAD_DIM, axis=1)
        gmix = mix_norm_g[l]
        ya, q, k, vt, u = _in_proj(
            xf, _pack_w_in(w_in[l]), row(gmlp_norm_g[l]), gmlp_ws[l], gbias,
            row(mla_q_norm_g[l]), *_pack_w_uq(mla_w_uq[l]), row(mla_kv_norm_g[l]), wk, wvt,
            cos_t, sin_t, row(gmix[:GM_WIDTH]), batch, seq)

        o = _attention(q, k, vt, batch, seq)

        abr, abi, bb = _s5_prep(ssm_a_re[l], ssm_a_im[l], ssm_log_dt[l], ssm_b_re[l], ssm_b_im[l], batch)
        u_tm = jnp.transpose(u.reshape(batch, seq, SSM_WIDTH), (1, 0, 2)).reshape(t, SSM_WIDTH)
        yc_tm = _s5(u_tm, abr, abi, bb, _pack_ssm_c(ssm_c_re[l], ssm_c_im[l]),
                    row(ssm_d[l]), ssm_glu_w[l].astype(BF16), row(ssm_glu_b[l]),
                    row(gmix[GM_WIDTH + MLA_WIDTH:]), batch)
        yc = jnp.transpose(yc_tm.reshape(seq, batch, SSM_WIDTH), (1, 0, 2)).reshape(t, SSM_WIDTH)

        wo = w_out[l].astype(BF16)
        xf = _out_ffn(xf, ya, o, yc, row(gmix[GM_WIDTH:GM_WIDTH + MLA_WIDTH]),
                      wo[:GM_WIDTH], wo[GM_WIDTH:GM_WIDTH + MLA_WIDTH], wo[GM_WIDTH + MLA_WIDTH:],
                      row(ln_g[l, 1]), row(ln_b[l, 1]),
                      ffn2_w_gate[l].astype(BF16), ffn2_w_up[l].astype(BF16),
                      (0.5 * ffn2_w_down[l]).astype(BF16), row(ln_g[l, 2]), row(ln_b[l, 2]), alpha)
    return xf.reshape(batch, seq, d)
```

```python
import functools
import math

import jax
import jax.numpy as jnp
from jax import lax
from jax.experimental import pallas as pl
from jax.experimental.pallas import tpu as pltpu

F32 = jnp.float32
BF16 = jnp.bfloat16

GM_HEADS = 4
GM_HEAD_DIM = 64
GM_WIDTH = GM_HEADS * GM_HEAD_DIM
GM_CHUNK = 128
MLA_HEADS = 8
MLA_NOPE = 64
MLA_ROPE = 32
MLA_V = 64
MLA_WIDTH = MLA_HEADS * MLA_V
Q_LORA = 256
KV_LORA = 128
ROPE_BASE = 10000.0
SSM_GROUPS = 16
SSM_GROUP_CH = 16
SSM_WIDTH = SSM_GROUPS * SSM_GROUP_CH
SSM_STATE = 64
SSM_NSTATE = SSM_GROUPS * SSM_STATE
LN_EPS = 1e-5
RMS_EPS = 1e-6
NEG_BIG = -1e30

LANES = 128
BF16_SUBLANES = 16
HEAD_PAD = LANES
MLA_VP = MLA_V + BF16_SUBLANES
SOFTMAX_C = (MLA_NOPE + MLA_ROPE) ** -0.5 * math.log2(math.e)
VMEM_LIMIT = 56 * 1024 * 1024

ROW_TILE = 512
ATTN_TQ = 512
ATTN_TK = 512
ATTN_CHUNK = ATTN_TK // 2
SSM_TL = 128
SSM_CHUNK = 32


def _gelu(x):
    c = math.sqrt(2.0 / math.pi)
    return 0.5 * x * (1.0 + jnp.tanh(c * (x + 0.044715 * (x * x * x))))


def _sigmoid(x):
    return 1.0 / (1.0 + jnp.exp(-x))


def _layer_norm(r, g, b):
    mu = jnp.mean(r, axis=-1, keepdims=True)
    d = r - mu
    var = jnp.mean(d * d, axis=-1, keepdims=True)
    return d * lax.rsqrt(var + LN_EPS) * g + b


def _rms_scale(y):
    return y * lax.rsqrt(jnp.mean(y * y, axis=-1, keepdims=True) + RMS_EPS)


def _dot(a, b):
    return jnp.dot(a, b, preferred_element_type=F32)


def _swiglu_ln(x, wg_ref, wu_ref, wd_ref, g, b, alpha, f_chunk):
    xb = x.astype(BF16)
    d_ff = wg_ref.shape[1]
    acts = []
    for c0 in range(0, d_ff, f_chunk):
        hg = _dot(xb, wg_ref[:, c0:c0 + f_chunk])
        hu = _dot(xb, wu_ref[:, c0:c0 + f_chunk])
        acts.append(((hg * _sigmoid(hg)) * hu).astype(BF16))
    half = x.shape[0] // 2
    outs = []
    for r0 in (0, half):
        y = None
        for ci, c0 in enumerate(range(0, d_ff, f_chunk)):
            part = _dot(acts[ci][r0:r0 + half, :], wd_ref[c0:c0 + f_chunk, :])
            y = part if y is None else y + part
        outs.append(_layer_norm(alpha * x[r0:r0 + half, :] + y, g, b))
    return jnp.concatenate(outs, axis=0)


def _ffn_ln_kernel(x_ref, wg_ref, wu_ref, wd_ref, g_ref, b_ref, o_ref, *, alpha, f_chunk):
    o_ref[...] = _swiglu_ln(x_ref[...], wg_ref, wu_ref, wd_ref, g_ref[...], b_ref[...], alpha, f_chunk)


def _resident(shape):
    nd = len(shape)
    return pl.BlockSpec(shape, lambda *_: (0,) * nd, pipeline_mode=pl.Buffered(1))


def _ffn_chunk(d_ff):
    return d_ff // 2 if (d_ff // 2) % LANES == 0 else d_ff


def _ffn_ln(x, wg, wu, wd, g, b, alpha):
    t, d = x.shape
    tm = ROW_TILE
    kern = functools.partial(_ffn_ln_kernel, alpha=alpha, f_chunk=_ffn_chunk(wg.shape[1]))
    return pl.pallas_call(
        kern,
        grid=(t // tm,),
        in_specs=[
            pl.BlockSpec((tm, d), lambda i: (i, 0)),
            _resident(wg.shape), _resident(wu.shape), _resident(wd.shape),
            _resident(g.shape), _resident(b.shape),
        ],
        out_specs=pl.BlockSpec((tm, d), lambda i: (i, 0)),
        out_shape=jax.ShapeDtypeStruct((t, d), F32),
        compiler_params=pltpu.CompilerParams(
            dimension_semantics=("arbitrary",), vmem_limit_bytes=VMEM_LIMIT),
        name="ffn_ln",
    )(x, wg, wu, wd, g, b)


def _rope_table_kernel(pos_ref, freq_ref, cos_ref, sin_ref):
    ang = pos_ref[...].astype(F32) * freq_ref[...]
    lane = lax.broadcasted_iota(jnp.int32, ang.shape, 1)
    half = MLA_ROPE // 2
    c = jnp.cos(ang)
    s = jnp.sin(ang)
    in_rope = (lane >= MLA_NOPE) & (lane < MLA_NOPE + MLA_ROPE)
    cos_ref[...] = jnp.where(lane < MLA_NOPE, 1.0, jnp.where(in_rope, c, 0.0))
    sin_ref[...] = jnp.where(in_rope, jnp.where(lane < MLA_NOPE + half, -s, s), 0.0)


def _rope_tables(positions):
    b, s = positions.shape
    t = b * s
    tm = ROW_TILE
    half = MLA_ROPE // 2
    inv_freq = 1.0 / (ROPE_BASE ** (jnp.arange(0, MLA_ROPE, 2, dtype=F32) / MLA_ROPE))
    freq_row = jnp.zeros((1, HEAD_PAD), F32)
    freq_row = freq_row.at[0, MLA_NOPE:MLA_NOPE + half].set(inv_freq)
    freq_row = freq_row.at[0, MLA_NOPE + half:MLA_NOPE + MLA_ROPE].set(inv_freq)
    pos = positions.reshape(t, 1)
    return pl.pallas_call(
        _rope_table_kernel,
        grid=(t // tm,),
        in_specs=[pl.BlockSpec((tm, 1), lambda i: (i, 0)),
                  pl.BlockSpec((1, HEAD_PAD), lambda i: (0, 0))],
        out_specs=[pl.BlockSpec((tm, HEAD_PAD), lambda i: (i, 0))] * 2,
        out_shape=[jax.ShapeDtypeStruct((t, HEAD_PAD), F32)] * 2,
        compiler_params=pltpu.CompilerParams(dimension_semantics=("arbitrary",)),
        name="rope_tables",
    )(pos, freq_row)


def _rope_tile(x, partner, cos_t, sin_t):
    return x * cos_t + partner * sin_t


def _in_proj_kernel(x_ref, win_ref, gng_ref, gws_ref, gbias_ref, qg_ref, wuq_ref, wuqp_ref, kvg_ref,
                    wuk_ref, wuvt_ref, vones_ref, cos_ref, sin_ref, gmix_a_ref,
                    ya_ref, qt_ref, k_ref, vt_ref, u_ref):
    tm = x_ref.shape[0]
    xb = x_ref[...].astype(BF16)
    h = _dot(xb, win_ref[...])
    o_v = GM_WIDTH
    o_q = 2 * GM_WIDTH
    o_kv = o_q + Q_LORA
    o_kr = o_kv + KV_LORA
    o_krp = o_kr + HEAD_PAD
    o_ssm = o_krp + HEAD_PAD

    u_ref[...] = h[:, o_ssm:o_ssm + SSM_WIDTH]

    ug = _gelu(h[:, 0:GM_WIDTH])
    vg = _gelu(h[:, o_v:o_v + GM_WIDTH])
    lane_g = lax.broadcasted_iota(jnp.int32, (1, GM_WIDTH), 1)
    head_masks = [(lane_g >= hd * GM_HEAD_DIM) & (lane_g < (hd + 1) * GM_HEAD_DIM)
                  for hd in range(GM_HEADS)]

    def seg_mean(a):
        out = jnp.zeros_like(a)
        for m in head_masks:
            s = jnp.sum(jnp.where(m, a, 0.0), axis=-1, keepdims=True) * (1.0 / GM_HEAD_DIM)
            out = jnp.where(m, s, out)
        return out

    dv = vg - seg_mean(vg)
    vn = dv * lax.rsqrt(seg_mean(dv * dv) + LN_EPS) * gng_ref[...]
    vnb = vn.astype(BF16)
    row = lax.broadcasted_iota(jnp.int32, (GM_CHUNK, GM_CHUNK), 0)
    col = lax.broadcasted_iota(jnp.int32, (GM_CHUNK, GM_CHUNK), 1)
    tril = col <= row
    wc = [jnp.where(tril, gws_ref[hd], 0.0).astype(BF16) for hd in range(GM_HEADS)]
    gbias = gbias_ref[...]
    gmix_a = gmix_a_ref[...]
    for c0 in range(0, tm, GM_CHUNK):
        vc = vnb[c0:c0 + GM_CHUNK, :]
        z = jnp.zeros((GM_CHUNK, GM_WIDTH), F32)
        for hd in range(GM_HEADS):
            z = jnp.where(head_masks[hd], _dot(wc[hd], vc), z)
        ya = ug[c0:c0 + GM_CHUNK, :] * (z + gbias)
        ya_ref[c0:c0 + GM_CHUNK, :] = (_rms_scale(ya) * gmix_a).astype(BF16)

    cos_t = cos_ref[...]
    sin_t = sin_ref[...]
    cq = h[:, o_q:o_q + Q_LORA]
    cqn = (_rms_scale(cq) * qg_ref[...]).astype(BF16)
    nt = (((1,), (1,)), ((), ()))
    qt = lax.dot_general(wuq_ref[...], cqn, nt, preferred_element_type=F32)
    qpt = lax.dot_general(wuqp_ref[...], cqn, nt, preferred_element_type=F32)
    cos_tt = cos_t.T
    sin_tt = sin_t.T
    for hd in range(MLA_HEADS):
        rows = slice(hd * HEAD_PAD, (hd + 1) * HEAD_PAD)
        qt_ref[0, 0, rows, :] = (_rope_tile(qt[rows, :], qpt[rows, :], cos_tt, sin_tt)
                                 * SOFTMAX_C).astype(BF16)
    ckv = h[:, o_kv:o_kv + KV_LORA]
    ckvn = (_rms_scale(ckv) * kvg_ref[...]).astype(BF16)
    kn = _dot(ckvn, wuk_ref[...])
    kr = _rope_tile(h[:, o_kr:o_kr + HEAD_PAD], h[:, o_krp:o_krp + HEAD_PAD], cos_t, sin_t)
    for hd in range(MLA_HEADS):
        k_ref[:, hd * HEAD_PAD:(hd + 1) * HEAD_PAD] = (kn[:, hd * HEAD_PAD:(hd + 1) * HEAD_PAD] + kr).astype(BF16)
    vt = lax.dot_general(wuvt_ref[...], ckvn, (((1,), (1,)), ((), ())),
                         preferred_element_type=F32)
    vt_ref[0, 0] = (vt + vones_ref[...]).astype(BF16)


def _in_proj(x, win, gng, gws, gbias, qg, wuq, wuqp, kvg, wuk, wuvt, cos_t, sin_t, gmix_a, batch, seq):
    t, d = x.shape
    tm = ROW_TILE
    spb = seq // tm
    row = lambda w: pl.BlockSpec((tm, w), lambda i: (i, 0))
    vrows = MLA_HEADS * MLA_VP
    is_one = (jnp.arange(vrows) % MLA_VP) == MLA_V
    vones = jnp.broadcast_to(is_one[:, None], (vrows, tm)).astype(F32)
    return pl.pallas_call(
        _in_proj_kernel,
        grid=(t // tm,),
        in_specs=[row(d), _resident(win.shape), _resident(gng.shape), _resident(gws.shape),
                  _resident(gbias.shape), _resident(qg.shape), _resident(wuq.shape),
                  _resident(wuqp.shape), _resident(kvg.shape), _resident(wuk.shape), _resident(wuvt.shape),
                  _resident(vones.shape), row(HEAD_PAD), row(HEAD_PAD), _resident(gmix_a.shape)],
        out_specs=[row(GM_WIDTH),
                   pl.BlockSpec((1, 1, MLA_HEADS * HEAD_PAD, tm), lambda i: (i // spb, i % spb, 0, 0)),
                   row(MLA_HEADS * HEAD_PAD),
                   pl.BlockSpec((1, 1, vrows, tm), lambda i: (i // spb, i % spb, 0, 0)),
                   row(SSM_WIDTH)],
        out_shape=[jax.ShapeDtypeStruct((t, GM_WIDTH), BF16),
                   jax.ShapeDtypeStruct((batch, spb, MLA_HEADS * HEAD_PAD, tm), BF16),
                   jax.ShapeDtypeStruct((t, MLA_HEADS * HEAD_PAD), BF16),
                   jax.ShapeDtypeStruct((batch, spb, vrows, tm), BF16),
                   jax.ShapeDtypeStruct((t, SSM_WIDTH), F32)],
        compiler_params=pltpu.CompilerParams(
            dimension_semantics=("arbitrary",), vmem_limit_bytes=VMEM_LIMIT),
        name="in_proj",
    )(x, win, gng, gws, gbias, qg, wuq, wuqp, kvg, wuk, wuvt, vones, cos_t, sin_t, gmix_a)


def _attn_kernel(q_ref, k_ref, vt_ref, o_ref, s0_ref, s1_ref, bm0_ref, bm1_ref,
                 m0_ref, m1_ref, acc0_ref, acc1_ref, *, tq, nq):
    tk = tq
    states = ((m0_ref, acc0_ref), (m1_ref, acc1_ref))

    def reset(st):
        m_ref, acc_ref = st
        m_ref[...] = jnp.full(m_ref.shape, NEG_BIG, F32)
        acc_ref[...] = jnp.zeros(acc_ref.shape, F32)

    def finalize(qb, st):
        _, acc_ref = st
        outs = []
        for hh in range(2):
            r0 = hh * MLA_VP
            inv_l = 1.0 / acc_ref[r0 + MLA_V:r0 + MLA_V + 1, :]
            outs.append(acc_ref[r0:r0 + MLA_V, :] * inv_l)
        o_ref[0, pl.ds(pl.multiple_of(qb * tq, tq), tq), :] = jnp.concatenate(outs, axis=0).T

    def stage(score, acc_from, st=None, done=None):
        masked = False
        score_to = None
        if acc_from is not None:
            m_ref, acc_ref = st
            ja, (sa_ref, bma_ref) = acc_from
            m_old = [m_ref[hh] for hh in range(2)]
            m_new = [jnp.maximum(m_old[hh], bma_ref[hh]) for hh in range(2)]
        if score is not None:
            qb_s, js, (ss_ref, bms_ref), masked = score
            score_to = score
            qs = [q_ref[0, qb_s, hh * HEAD_PAD:(hh + 1) * HEAD_PAD, :] for hh in range(2)]
            start = pl.multiple_of(js * tk, tk)
            bmax = [None, None]
        order = [(kind, c0) for c0 in range(0, tk, ATTN_CHUNK) for kind in ("s", "a")]
        for kind, c0 in order:
            for hh in range(2):
                rows = slice(hh * MLA_VP, (hh + 1) * MLA_VP)
                if kind == "s" and score_to is not None:
                    kb = k_ref[0, pl.ds(start + c0, ATTN_CHUNK), hh * HEAD_PAD:(hh + 1) * HEAD_PAD]
                    s = _dot(kb, qs[hh])
                    if masked:
                        krow = lax.broadcasted_iota(jnp.int32, s.shape, 0) + c0
                        qcol = lax.broadcasted_iota(jnp.int32, s.shape, 1)
                        s = jnp.where(krow <= qcol, s, NEG_BIG)
                    ss_ref[hh, c0:c0 + ATTN_CHUNK, :] = s
                    cmax = jnp.max(s, axis=0, keepdims=True)
                    bmax[hh] = cmax if bmax[hh] is None else jnp.maximum(bmax[hh], cmax)
                if kind == "a" and acc_from is not None:
                    p = jnp.exp2(sa_ref[hh, c0:c0 + ATTN_CHUNK, :] - m_new[hh]).astype(BF16)
                    d = _dot(vt_ref[0, ja, rows, c0:c0 + ATTN_CHUNK], p)
                    if c0 == 0:
                        acc_ref[rows, :] = jnp.exp2(m_old[hh] - m_new[hh]) * acc_ref[rows, :] + d
                    else:
                        acc_ref[rows, :] += d
        for hh in range(2):
            if acc_from is not None:
                m_ref[hh] = m_new[hh]
            if score_to is not None:
                bms_ref[hh] = bmax[hh]
        if done is not None:
            finalize(*done)
            reset(done[1])

    bufs = ((s0_ref, bm0_ref), (s1_ref, bm1_ref))

    diag_buf = (0, 1, 1, 0)
    reset(states[0])
    reset(states[1])
    stage((0, 0, bufs[0], True), None)

    def four_q_blocks(g, carry):
        for r in range(4):
            qb = 4 * g + r
            x = diag_buf[r]
            st = states[r % 2]

            def pair(t, c, qb=qb, x=x, st=st):
                stage((qb, 2 * t, bufs[x ^ 1], False), (jnp.where(t == 0, qb, 2 * t - 1), bufs[x]), st)
                stage((qb, 2 * t + 1, bufs[x], False), (2 * t, bufs[x ^ 1]), st)
                return c

            lax.fori_loop(0, qb // 2, pair, 0)
            if r % 2 == 1:
                stage((qb, qb - 1, bufs[x ^ 1], False), (jnp.where(qb == 1, qb, qb - 2), bufs[x]), st)
                last_buf = x ^ 1
            else:
                last_buf = x
            assert diag_buf[(r + 1) % 4] == last_buf ^ 1
            stage((jnp.minimum(qb + 1, nq - 1), jnp.minimum(qb + 1, nq - 1), bufs[last_buf ^ 1], True),
                  (jnp.where(qb == 0, qb, qb - 1), bufs[last_buf]), st,
                  done=(qb - 1, states[(r - 1) % 2]) if r > 0 else None)
            if r == 3:
                finalize(qb, st)
                reset(st)
        return carry

    lax.fori_loop(0, nq // 4, four_q_blocks, 0)


def _attention(qt, k, vt, batch, seq):
    tq, tk = ATTN_TQ, ATTN_TK
    nq = seq // tq
    assert nq % 4 == 0
    k3 = k.reshape(batch, seq, MLA_HEADS * HEAD_PAD)
    kern = functools.partial(_attn_kernel, tq=tq, nq=nq)
    o = pl.pallas_call(
        kern,
        grid=(batch, MLA_HEADS // 2),
        in_specs=[pl.BlockSpec((1, nq, 2 * HEAD_PAD, tq), lambda b, h: (b, 0, h, 0)),
                  pl.BlockSpec((1, seq, 2 * HEAD_PAD), lambda b, h: (b, 0, h)),
                  pl.BlockSpec((1, seq // tk, 2 * MLA_VP, tk), lambda b, h: (b, 0, h, 0))],
        out_specs=pl.BlockSpec((1, seq, 2 * MLA_V), lambda b, h: (b, 0, h)),
        out_shape=jax.ShapeDtypeStruct((batch, seq, MLA_WIDTH), F32),
        scratch_shapes=[pltpu.VMEM((2, tk, tq), F32), pltpu.VMEM((2, tk, tq), F32),
                        pltpu.VMEM((2, 1, tq), F32), pltpu.VMEM((2, 1, tq), F32),
                        pltpu.VMEM((2, 1, tq), F32), pltpu.VMEM((2, 1, tq), F32),
                        pltpu.VMEM((2 * MLA_VP, tq), F32), pltpu.VMEM((2 * MLA_VP, tq), F32)],
        compiler_params=pltpu.CompilerParams(
            dimension_semantics=("arbitrary", "arbitrary"), vmem_limit_bytes=VMEM_LIMIT),
        name="attention",
    )(qt, k3, vt)
    return o.reshape(batch * seq, MLA_WIDTH)


def _s5_prep_kernel(ar_ref, ai_ref, dt_ref, bre_ref, bim_ref, abr_ref, abi_ref, bb_ref):
    ar = ar_ref[...]
    ai = ai_ref[...]
    dt = jnp.exp(dt_ref[...])
    mag = jnp.exp(ar * dt)
    abr = mag * jnp.cos(ai * dt)
    abi = mag * jnp.sin(ai * dt)
    den = ar * ar + ai * ai
    cr = ((abr - 1.0) * ar + abi * ai) / den
    ci = (abi * ar - (abr - 1.0) * ai) / den
    abr_ref[...] = jnp.broadcast_to(abr, abr_ref.shape)
    abi_ref[...] = jnp.broadcast_to(abi, abi_ref.shape)
    bre = bre_ref[...]
    bim = bim_ref[...]
    r = lax.broadcasted_iota(jnp.int32, bre.shape, 0) // SSM_GROUP_CH
    c = lax.broadcasted_iota(jnp.int32, bre.shape, 1) // SSM_STATE
    same = r == c
    bb_ref[:, 0:SSM_NSTATE] = jnp.where(same, cr * bre - ci * bim, 0.0).astype(BF16)
    bb_ref[:, SSM_NSTATE:2 * SSM_NSTATE] = jnp.where(same, cr * bim + ci * bre, 0.0).astype(BF16)


def _s5_prep(a_re, a_im, log_dt, b_re, b_im, batch):
    n = SSM_NSTATE
    ar = a_re.reshape(1, n)
    ai = a_im.reshape(1, n)
    dt = jnp.repeat(log_dt, SSM_STATE).reshape(1, n)
    def expand(bm):
        cols = jnp.transpose(bm, (2, 0, 1)).reshape(SSM_GROUP_CH, n)
        return jnp.tile(cols, (SSM_GROUPS, 1))
    whole = lambda shp: pl.BlockSpec(shp, lambda: (0,) * len(shp))
    return pl.pallas_call(
        _s5_prep_kernel,
        in_specs=[whole((1, n)), whole((1, n)), whole((1, n)),
                  whole((SSM_WIDTH, n)), whole((SSM_WIDTH, n))],
        out_specs=[whole((batch, n)), whole((batch, n)), whole((SSM_WIDTH, 2 * n))],
        out_shape=[jax.ShapeDtypeStruct((batch, n), F32), jax.ShapeDtypeStruct((batch, n), F32),
                   jax.ShapeDtypeStruct((SSM_WIDTH, 2 * n), BF16)],
        name="s5_prep",
    )(ar, ai, dt, expand(b_re), expand(b_im))


def _s5_kernel(u_ref, abr_ref, abi_ref, bb_ref, cc_ref, d_ref, gw_ref, gb_ref, gmix_ref,
               y_ref, hre_ref, him_ref, cre_ref, cim_ref, *, batch, tl, ch):
    n = SSM_NSTATE
    rc = ch * batch
    nch = tl // ch

    @pl.when(pl.program_id(0) == 0)
    def _():
        cre_ref[...] = jnp.zeros(cre_ref.shape, F32)
        cim_ref[...] = jnp.zeros(cim_ref.shape, F32)

    ar = abr_ref[...]
    ai = abi_ref[...]

    def expand(k):
        r = slice(k * rc, (k + 1) * rc)
        ub = u_ref[r, :].astype(BF16)
        hre_ref[r, :] = _dot(ub, bb_ref[:, 0:n])
        him_ref[r, :] = _dot(ub, bb_ref[:, n:2 * n])

    def scan(k, hr, hi):
        for t in range(ch):
            r = slice(k * rc + t * batch, k * rc + (t + 1) * batch)
            nr = ar * hr - ai * hi + hre_ref[r, :]
            ni = ar * hi + ai * hr + him_ref[r, :]
            hre_ref[r, :] = nr
            him_ref[r, :] = ni
            hr, hi = nr, ni
        return hr, hi

    def contract(k):
        r = slice(k * rc, (k + 1) * rc)
        y = (_dot(hre_ref[r, :].astype(BF16), cc_ref[0:n, :])
             + _dot(him_ref[r, :].astype(BF16), cc_ref[n:2 * n, :])
             + d_ref[...] * u_ref[r, :])
        y = _gelu(y)
        y = y * _sigmoid(_dot(y.astype(BF16), gw_ref[...]) + gb_ref[...])
        y_ref[r, :] = (_rms_scale(y) * gmix_ref[...]).astype(BF16)

    hr, hi = cre_ref[...], cim_ref[...]
    expand(0)
    for k in range(nch):
        if k + 1 < nch:
            expand(k + 1)
        hr, hi = scan(k, hr, hi)
        if k >= 1:
            contract(k - 1)
    contract(nch - 1)
    cre_ref[...] = hr
    cim_ref[...] = hi


def _s5(u_tm, abr, abi, bb, cc, dskip, gw, gb, gmix_c, batch):
    rows, w = u_tm.shape
    tl = SSM_TL
    tr = tl * batch
    n = SSM_NSTATE
    kern = functools.partial(_s5_kernel, batch=batch, tl=tl, ch=SSM_CHUNK)
    return pl.pallas_call(
        kern,
        grid=(rows // tr,),
        in_specs=[pl.BlockSpec((tr, w), lambda i: (i, 0)),
                  _resident(abr.shape), _resident(abi.shape), _resident(bb.shape),
                  _resident(cc.shape), _resident(dskip.shape), _resident(gw.shape),
                  _resident(gb.shape), _resident(gmix_c.shape)],
        out_specs=pl.BlockSpec((tr, w), lambda i: (i, 0)),
        out_shape=jax.ShapeDtypeStruct((rows, w), BF16),
        scratch_shapes=[pltpu.VMEM((tr, n), F32), pltpu.VMEM((tr, n), F32),
                        pltpu.VMEM((batch, n), F32), pltpu.VMEM((batch, n), F32)],
        compiler_params=pltpu.CompilerParams(
            dimension_semantics=("arbitrary",), vmem_limit_bytes=VMEM_LIMIT),
        name="s5",
    )(u_tm, abr, abi, bb, cc, dskip, gw, gb, gmix_c)


def _out_ffn_kernel(x_ref, ya_ref, o_ref, yc_ref, gmix_b_ref, wa_ref, wb_ref, wc_ref,
                    g1_ref, b1_ref, wg_ref, wu_ref, wd_ref, g2_ref, b2_ref, out_ref,
                    *, alpha, f_chunk):
    half = x_ref.shape[0] // 2
    parts = []
    for r0 in (0, half):
        r = slice(r0, r0 + half)
        yb = (_rms_scale(o_ref[r, :]) * gmix_b_ref[...]).astype(BF16)
        y = _dot(ya_ref[r, :], wa_ref[...]) + _dot(yb, wb_ref[...]) + _dot(yc_ref[r, :], wc_ref[...])
        parts.append(_layer_norm(alpha * x_ref[r, :] + y, g1_ref[...], b1_ref[...]))
    x1 = jnp.concatenate(parts, axis=0)
    out_ref[...] = _swiglu_ln(x1, wg_ref, wu_ref, wd_ref, g2_ref[...], b2_ref[...], alpha, f_chunk)


def _out_ffn(x, ya, o, yc, gmix_b, wa, wb, wc, g1, b1, wg, wu, wd, g2, b2, alpha):
    t, d = x.shape
    tm = ROW_TILE
    row = lambda w: pl.BlockSpec((tm, w), lambda i: (i, 0))
    kern = functools.partial(_out_ffn_kernel, alpha=alpha, f_chunk=_ffn_chunk(wg.shape[1]))
    return pl.pallas_call(
        kern,
        grid=(t // tm,),
        in_specs=[row(d), row(GM_WIDTH), row(MLA_WIDTH), row(SSM_WIDTH),
                  _resident(gmix_b.shape), _resident(wa.shape), _resident(wb.shape),
                  _resident(wc.shape), _resident(g1.shape), _resident(b1.shape),
                  _resident(wg.shape), _resident(wu.shape), _resident(wd.shape),
                  _resident(g2.shape), _resident(b2.shape)],
        out_specs=row(d),
        out_shape=jax.ShapeDtypeStruct((t, d), F32),
        compiler_params=pltpu.CompilerParams(
            dimension_semantics=("arbitrary",), vmem_limit_bytes=VMEM_LIMIT),
        name="out_ffn",
    )(x, ya, o, yc, gmix_b, wa, wb, wc, g1, b1, wg, wu, wd, g2, b2)


def _pack_w_in(w_in):
    d = w_in.shape[0]
    half = MLA_ROPE // 2
    o1 = 2 * GM_WIDTH
    o2 = o1 + Q_LORA
    o3 = o2 + KV_LORA
    o4 = o3 + MLA_ROPE
    zeros = jnp.zeros((d, HEAD_PAD), w_in.dtype)
    kr_tile = zeros.at[:, MLA_NOPE:MLA_NOPE + MLA_ROPE].set(w_in[:, o3:o4])
    swapped = jnp.concatenate([w_in[:, o3 + half:o4], w_in[:, o3:o3 + half]], axis=1)
    krp_tile = zeros.at[:, MLA_NOPE:MLA_NOPE + MLA_ROPE].set(swapped)
    return jnp.concatenate([w_in[:, :o3], kr_tile, krp_tile, w_in[:, o4:]], axis=1).astype(BF16)


def _pack_w_uq(w_uq):
    r = w_uq.shape[0]
    half = MLA_ROPE // 2
    w = w_uq.reshape(r, MLA_HEADS, MLA_NOPE + MLA_ROPE)
    pad = HEAD_PAD - MLA_NOPE - MLA_ROPE
    wq = jnp.pad(w, ((0, 0), (0, 0), (0, pad)))
    swapped = jnp.concatenate([w[:, :, MLA_NOPE + half:], w[:, :, MLA_NOPE:MLA_NOPE + half]], axis=2)
    wp = jnp.pad(swapped, ((0, 0), (0, 0), (MLA_NOPE, pad)))
    shape = (r, MLA_HEADS * HEAD_PAD)
    return wq.reshape(shape).T.astype(BF16), wp.reshape(shape).T.astype(BF16)


def _pack_w_ukv(w_ukv):
    r = w_ukv.shape[0]
    w = w_ukv.reshape(r, MLA_HEADS, MLA_NOPE + MLA_V)
    wk = jnp.pad(w[:, :, :MLA_NOPE], ((0, 0), (0, 0), (0, HEAD_PAD - MLA_NOPE)))
    wvt = jnp.pad(jnp.transpose(w[:, :, MLA_NOPE:], (1, 2, 0)), ((0, 0), (0, MLA_VP - MLA_V), (0, 0)))
    return (wk.reshape(r, MLA_HEADS * HEAD_PAD).astype(BF16),
            wvt.reshape(MLA_HEADS * MLA_VP, r).astype(BF16))


def _pack_ssm_c(c_re, c_im):
    def blockdiag(cm):
        eye = jnp.eye(SSM_GROUPS, dtype=cm.dtype)
        m = jnp.transpose(cm, (0, 2, 1))[:, :, None, :] * eye[:, None, :, None]
        return m.reshape(SSM_NSTATE, SSM_WIDTH)
    return jnp.concatenate([blockdiag(c_re), -blockdiag(c_im)], axis=0).astype(BF16)


def kernel(x, positions, ln_g, ln_b, ffn1_w_gate, ffn1_w_up, ffn1_w_down, w_in, gmlp_norm_g, gmlp_ws, gmlp_bs, mla_q_norm_g, mla_w_uq, mla_kv_norm_g, mla_w_ukv, ssm_a_re, ssm_a_im, ssm_b_re, ssm_b_im, ssm_c_re, ssm_c_im, ssm_d, ssm_log_dt, ssm_glu_w, ssm_glu_b, mix_norm_g, w_out, ffn2_w_gate, ffn2_w_up, ffn2_w_down):
    batch, seq, d = x.shape
    depth = w_in.shape[0]
    t = batch * seq
    assert batch == 8, "the S5 kernel keeps the batch on the 8 sublanes of a vreg"
    assert seq % max(ROW_TILE, ATTN_TQ, SSM_TL) == 0 and ROW_TILE % GM_CHUNK == 0
    assert ATTN_TQ == ATTN_TK == ROW_TILE
    alpha = (2 * depth) ** 0.25

    cos_t, sin_t = _rope_tables(positions)
    xf = x.reshape(t, d)
    row = lambda a: a.reshape(1, -1)
    for l in range(depth):
        xf = _ffn_ln(xf, ffn1_w_gate[l].astype(BF16), ffn1_w_up[l].astype(BF16),
                     (0.5 * ffn1_w_down[l]).astype(BF16), row(ln_g[l, 0]), row(ln_b[l, 0]), alpha)

        wk, wvt = _pack_w_ukv(mla_w_ukv[l])
        gbias = jnp.repeat(gmlp_bs[l].T, GM_HEAD_DIM, axis=1)
        gmix = mix_norm_g[l]
        ya, q, k, vt, u = _in_proj(
            xf, _pack_w_in(w_in[l]), row(gmlp_norm_g[l]), gmlp_ws[l], gbias,
            row(mla_q_norm_g[l]), *_pack_w_uq(mla_w_uq[l]), row(mla_kv_norm_g[l]), wk, wvt,
            cos_t, sin_t, row(gmix[:GM_WIDTH]), batch, seq)

        o = _attention(q, k, vt, batch, seq)

        abr, abi, bb = _s5_prep(ssm_a_re[l], ssm_a_im[l], ssm_log_dt[l], ssm_b_re[l], ssm_b_im[l], batch)
        u_tm = jnp.transpose(u.reshape(batch, seq, SSM_WIDTH), (1, 0, 2)).reshape(t, SSM_WIDTH)
        yc_tm = _s5(u_tm, abr, abi, bb, _pack_ssm_c(ssm_c_re[l], ssm_c_im[l]),
                    row(ssm_d[l]), ssm_glu_w[l].astype(BF16), row(ssm_glu_b[l]),
                    row(gmix[GM_WIDTH + MLA_WIDTH:]), batch)
        yc = jnp.transpose(yc_tm.reshape(seq, batch, SSM_WIDTH), (1, 0, 2)).reshape(t, SSM_WIDTH)

        wo = w_out[l].astype(BF16)
        xf = _out_ffn(xf, ya, o, yc, row(gmix[GM_WIDTH:GM_WIDTH + MLA_WIDTH]),
                      wo[:GM_WIDTH], wo[GM_WIDTH:GM_WIDTH + MLA_WIDTH], wo[GM_WIDTH + MLA_WIDTH:],
                      row(ln_g[l, 1]), row(ln_b[l, 1]),
                      ffn2_w_gate[l].astype(BF16), ffn2_w_up[l].astype(BF16),
                      (0.5 * ffn2_w_down[l]).astype(BF16), row(ln_g[l, 2]), row(ln_b[l, 2]), alpha)
    return xf.reshape(batch, seq, d)
```
